```python
import jax, jax.numpy as jnp
from jax import lax
import numpy as np

D_MODEL = 2048
BATCH = 4
SEQ = 2048
DEPTH = 2
DEC_BATCH = 128
DEC_SEQ = 8
PAST_LEN = 16384
PAGE_SIZE = 128

D_POOL = 1024
POOL_WINDOWS = (2, 4, 8, 16)
POOL_GC = D_POOL // len(POOL_WINDOWS)
POOL_BUF = max(POOL_WINDOWS) - 1
N_HEADS = 8
HEAD_K = 128
HEAD_V = 128
D_QK = N_HEADS * HEAD_K
D_V = N_HEADS * HEAD_V
D_QKV = 2 * D_QK + D_V
CONV_K = 4
CHUNK = 64
IN_SIZES = (D_POOL, D_QKV, D_V, N_HEADS, N_HEADS, D_MODEL, D_MODEL)
IN_COLS = sum(IN_SIZES)
D_FF = 5632
N_EXPERTS = 8
TOP_K = 2
D_FF_EXPERT = 7168
N_DENSE = (DEPTH + 1) // 2
N_MOE = DEPTH // 2
EPS = 1e-6

kernel_name = "pool_deltanet_gated_hybrid_step"

F32 = jnp.float32


def rmsnorm(x, w):
    xf = x.astype(F32)
    y = xf * lax.rsqrt(jnp.mean(xf * xf, axis=-1, keepdims=True) + EPS)
    return (y * w.astype(F32)).astype(x.dtype)


def l2norm(x):
    return x * lax.rsqrt(jnp.sum(x * x, axis=-1, keepdims=True) + EPS)


def split_proj(proj):
    idx = np.cumsum(IN_SIZES)[:-1].tolist()
    return jnp.split(proj, idx, axis=-1)


def pool_mixer(u_ext, pos, pool_w, pool_scale):
    B, T = u_ext.shape[0], pos.shape[0]
    uf = u_ext.astype(F32)
    cs = jnp.cumsum(uf, axis=1)
    cs = jnp.concatenate([jnp.zeros_like(cs[:, :1]), cs], axis=1)
    end = cs[:, POOL_BUF + 1:]
    cur = uf[:, POOL_BUF:]
    outs = []
    for gi, w in enumerate(POOL_WINDOWS):
        sl = slice(gi * POOL_GC, (gi + 1) * POOL_GC)
        start = cs[:, POOL_BUF + 1 - w:POOL_BUF + 1 - w + T, sl]
        cnt = jnp.minimum(pos + 1, w).astype(F32)[None, :, None]
        outs.append((end[..., sl] - start) / cnt - cur[..., sl])
    d = jnp.stack(outs, axis=2)
    y = jnp.einsum('btgc,gcd->btgd', d, pool_w.astype(F32)).reshape(B, T, D_POOL)
    return y * pool_scale.astype(F32)


def short_conv(x_ext, conv_w):
    T = x_ext.shape[1] - (CONV_K - 1)
    y = x_ext[:, 0:T] * conv_w[0]
    for j in range(1, CONV_K):
        y = y + x_ext[:, j:j + T] * conv_w[j]
    return jax.nn.silu(y)


def gated_delta(q, k, v, g, beta, s0):
    B, T, H, _ = q.shape
    C = min(CHUNK, T)
    pad = (-T) % C
    N = (T + pad) // C

    def chunks(x):
        x = jnp.pad(x, [(0, 0), (0, pad)] + [(0, 0)] * (x.ndim - 2))
        x = x.reshape((B, N, C) + x.shape[2:])
        return x.transpose((1, 0, 3, 2) + tuple(range(4, x.ndim)))

    qc, kc, vc, gc, bc = chunks(q), chunks(k), chunks(v), chunks(g), chunks(beta)
    gcum = jnp.cumsum(gc, axis=-1)
    tri = jnp.tril(jnp.ones((C, C), bool))
    strict = jnp.tril(jnp.ones((C, C), bool), -1)
    diff = gcum[..., :, None] - gcum[..., None, :]
    decay = jnp.where(tri, jnp.exp(jnp.where(tri, diff, 0.0)), 0.0)
    kb = kc * bc[..., None]
    low = jnp.where(strict, jnp.einsum('nbhik,nbhjk->nbhij', kb, kc) * decay, 0.0)
    unit = low + jnp.eye(C, dtype=low.dtype)

    def solve(rhs):
        return lax.linalg.triangular_solve(unit, rhs, left_side=True, lower=True, unit_diagonal=True)

    u = solve(vc * bc[..., None])
    w = solve(kb * jnp.exp(gcum)[..., None])
    a_qk = jnp.where(tri, jnp.einsum('nbhik,nbhjk->nbhij', qc, kc) * decay, 0.0)

    def step(S, xs):
        qi, ki, ui, wi, gi, ai = xs
        v_new = ui - jnp.einsum('bhck,bhkv->bhcv', wi, S)
        o = (jnp.einsum('bhck,bhkv->bhcv', qi * jnp.exp(gi)[..., None], S)
             + jnp.einsum('bhij,bhjv->bhiv', ai, v_new))
        g_last = gi[..., -1:]
        S = (S * jnp.exp(g_last)[..., None]
             + jnp.einsum('bhck,bhcv->bhkv', ki * jnp.exp(g_last - gi)[..., None], v_new))
        return S, o

    S, o = lax.scan(step, s0, (qc, kc, u, w, gcum, a_qk))
    o = o.transpose(1, 0, 3, 2, 4).reshape(B, N * C, H, -1)[:, :T]
    return o, S


def mixer_block(a, pos, pool_buf, conv_buf, rec, w_in, pool_w, pool_scale, conv_w, a_log,
                dt_bias, o_norm, w_branch_a, w_branch_b, w_out):
    B, T, _ = a.shape
    proj = a @ w_in
    u_a, qkv, z, a_in, b_in, gate_a, gate_b = split_proj(proj)
    u_ext = jnp.concatenate([pool_buf.astype(a.dtype), u_a], axis=1)
    y_a = pool_mixer(u_ext, pos, pool_w, pool_scale).astype(a.dtype)
    qkv_ext = jnp.concatenate([conv_buf.astype(a.dtype), qkv], axis=1)
    qkv_c = short_conv(qkv_ext, conv_w).astype(F32)
    q, k, v = jnp.split(qkv_c, [D_QK, 2 * D_QK], axis=-1)
    q = l2norm(q.reshape(B, T, N_HEADS, HEAD_K)) * (HEAD_K ** -0.5)
    k = l2norm(k.reshape(B, T, N_HEADS, HEAD_K))
    v = v.reshape(B, T, N_HEADS, HEAD_V)
    beta = jax.nn.sigmoid(b_in.astype(F32))
    g = -jnp.exp(a_log.astype(F32)) * jax.nn.softplus(a_in.astype(F32) + dt_bias.astype(F32))
    o, S = gated_delta(q, k, v, g, beta, rec.astype(F32))
    o = rmsnorm(o, o_norm) * jax.nn.silu(z.astype(F32).reshape(B, T, N_HEADS, HEAD_V))
    y_b = o.reshape(B, T, D_V).astype(a.dtype)
    merged = jax.nn.sigmoid(gate_a) * (y_a @ w_branch_a) + jax.nn.sigmoid(gate_b) * (y_b @ w_branch_b)
    new_pool = u_ext[:, -POOL_BUF:].astype(pool_buf.dtype)
    new_conv = qkv_ext[:, -(CONV_K - 1):].astype(conv_buf.dtype)
    return merged @ w_out, new_pool, new_conv, S.astype(rec.dtype)


def swiglu(h, wg, wu, wd):
    return (jax.nn.silu(h @ wg) * (h @ wu)) @ wd


def moe_swiglu(h, router, wg, wu, wd):
    logits = (h @ router).astype(F32)
    top_v, top_i = lax.top_k(logits, TOP_K)
    gates = jax.nn.softmax(top_v, axis=-1)
    comb = jnp.sum(jax.nn.one_hot(top_i, N_EXPERTS, dtype=F32) * gates[..., None], axis=-2)
    y = jnp.zeros_like(h)
    for e in range(N_EXPERTS):
        y = y + comb[..., e:e + 1].astype(h.dtype) * swiglu(h, wg[e], wu[e], wd[e])
    return y


def run_group(x, pos, pool_bufs, conv_bufs, rec_states, norm_mix, w_in, pool_w, pool_scale,
              conv_w, a_log, dt_bias, o_norm, w_branch_a, w_branch_b, w_out, norm_ffn,
              ffn_w_gate, ffn_w_up, ffn_w_down, router, moe_w_gate, moe_w_up, moe_w_down,
              norm_final):
    h = x
    new_pool, new_conv, new_rec = [], [], []
    for layer in range(DEPTH):
        a = rmsnorm(h, norm_mix[layer])
        mix, pb, cb, S = mixer_block(a, pos, pool_bufs[layer], conv_bufs[layer], rec_states[layer],
                                     w_in[layer], pool_w[layer], pool_scale[layer], conv_w[layer],
                                     a_log[layer], dt_bias[layer], o_norm[layer],
                                     w_branch_a[layer], w_branch_b[layer], w_out[layer])
        h = h + mix
        f = rmsnorm(h, norm_ffn[layer])
        j = layer // 2
        if layer % 2 == 0:
            h = h + swiglu(f, ffn_w_gate[j], ffn_w_up[j], ffn_w_down[j])
        else:
            h = h + moe_swiglu(f, router[j], moe_w_gate[j], moe_w_up[j], moe_w_down[j])
        new_pool.append(pb)
        new_conv.append(cb)
        new_rec.append(S)
    y = rmsnorm(h, norm_final)
    return y, jnp.stack(new_pool), jnp.stack(new_conv), jnp.stack(new_rec)


def setup_inputs(seed: int = 0) -> dict:
    key = jax.random.key(seed)
    ks = jax.random.split(key, 32)

    def nrm(k, shape, scale):
        return jax.random.normal(k, shape, F32) * scale

    dt = jnp.exp(jax.random.uniform(ks[12], (DEPTH, N_HEADS), F32, np.log(1e-3), np.log(0.1)))
    return {
        "x_prompt": nrm(ks[0], (BATCH, SEQ, D_MODEL), 1.0),
        "x_sample": nrm(ks[1], (DEC_BATCH, DEC_SEQ, D_MODEL), 1.0),
        "state_pool": nrm(ks[2], (DEPTH, DEC_BATCH, POOL_BUF, D_POOL), 1.0),
        "state_conv": nrm(ks[3], (DEPTH, DEC_BATCH, CONV_K - 1, D_QKV), 1.0),
        "state_rec": nrm(ks[4], (DEPTH, DEC_BATCH, N_HEADS, HEAD_K, HEAD_V), HEAD_K ** -0.5),
        "norm_mix": 1.0 + nrm(ks[5], (DEPTH, D_MODEL), 0.02),
        "w_in": nrm(ks[6], (DEPTH, D_MODEL, IN_COLS), D_MODEL ** -0.5),
        "pool_w": nrm(ks[7], (DEPTH, len(POOL_WINDOWS), POOL_GC, POOL_GC), POOL_GC ** -0.5),
        "pool_scale": 1.0 + nrm(ks[8], (DEPTH, D_POOL), 0.1),
        "conv_w": nrm(ks[9], (DEPTH, CONV_K, D_QKV), CONV_K ** -0.5),
        "a_log": jnp.log(jax.random.uniform(ks[10], (DEPTH, N_HEADS), F32, 1.0, 16.0)),
        "dt_bias": dt + jnp.log(-jnp.expm1(-dt)),
        "o_norm": 1.0 + nrm(ks[11], (DEPTH, HEAD_V), 0.02),
        "w_branch_a": nrm(ks[13], (DEPTH, D_POOL, D_MODEL), D_POOL ** -0.5),
        "w_branch_b": nrm(ks[14], (DEPTH, D_V, D_MODEL), D_V ** -0.5),
        "w_out": nrm(ks[15], (DEPTH, D_MODEL, D_MODEL), D_MODEL ** -0.5),
        "norm_ffn": 1.0 + nrm(ks[16], (DEPTH, D_MODEL), 0.02),
        "ffn_w_gate": nrm(ks[17], (N_DENSE, D_MODEL, D_FF), D_MODEL ** -0.5),
        "ffn_w_up": nrm(ks[18], (N_DENSE, D_MODEL, D_FF), D_MODEL ** -0.5),
        "ffn_w_down": nrm(ks[19], (N_DENSE, D_FF, D_MODEL), D_FF ** -0.5),
        "router": nrm(ks[20], (N_MOE, D_MODEL, N_EXPERTS), D_MODEL ** -0.5),
        "moe_w_gate": nrm(ks[21], (N_MOE, N_EXPERTS, D_MODEL, D_FF_EXPERT), D_MODEL ** -0.5),
        "moe_w_up": nrm(ks[22], (N_MOE, N_EXPERTS, D_MODEL, D_FF_EXPERT), D_MODEL ** -0.5),
        "moe_w_down": nrm(ks[23], (N_MOE, N_EXPERTS, D_FF_EXPERT, D_MODEL), D_FF_EXPERT ** -0.5),
        "norm_final": 1.0 + nrm(ks[24], (D_MODEL,), 0.02),
    }


def reference(x_prompt, x_sample, state_pool, state_conv, state_rec, norm_mix, w_in, pool_w,
              pool_scale, conv_w, a_log, dt_bias, o_norm, w_branch_a, w_branch_b, w_out, norm_ffn,
              ffn_w_gate, ffn_w_up, ffn_w_down, router, moe_w_gate, moe_w_up, moe_w_down,
              norm_final):
    weights = (norm_mix, w_in, pool_w, pool_scale, conv_w, a_log, dt_bias, o_norm, w_branch_a,
               w_branch_b, w_out, norm_ffn, ffn_w_gate, ffn_w_up, ffn_w_down, router,
               moe_w_gate, moe_w_up, moe_w_down, norm_final)
    pos_p = jnp.arange(SEQ, dtype=jnp.int32)
    zp = jnp.zeros((DEPTH, BATCH, POOL_BUF, D_POOL), state_pool.dtype)
    zc = jnp.zeros((DEPTH, BATCH, CONV_K - 1, D_QKV), state_conv.dtype)
    zr = jnp.zeros((DEPTH, BATCH, N_HEADS, HEAD_K, HEAD_V), state_rec.dtype)
    y_prompt, pool_p, conv_p, rec_p = run_group(x_prompt, pos_p, zp, zc, zr, *weights)
    pos_s = PAST_LEN + jnp.arange(DEC_SEQ, dtype=jnp.int32)
    y_sample, pool_s, conv_s, rec_s = run_group(x_sample, pos_s, state_pool, state_conv,
                                                state_rec, *weights)
    return (y_prompt, y_sample, pool_p, conv_p, rec_p, pool_s, conv_s, rec_s)
```

```python
import functools

import jax
import jax.numpy as jnp
import numpy as np
from jax import lax
from jax.experimental import pallas as pl
from jax.experimental.pallas import tpu as pltpu

F32 = jnp.float32
BF16 = jnp.bfloat16
EPS = 1e-6
POOL_WINDOWS = (2, 4, 8, 16)
POOL_HALO = 16
CONV_HALO = 8
CHUNK = 64
TOP_K = 2
PAST_LEN = 16384
LANES = 128
SUB_ROWS = 256
VMEM_BYTES = 56 * 1024 * 1024


def _params(n_axes, vmem=VMEM_BYTES):
    return pltpu.CompilerParams(dimension_semantics=("arbitrary",) * n_axes, vmem_limit_bytes=vmem)


def _pick(n, pref, mult):
    if n <= pref:
        return n
    for c in range(pref, 0, -1):
        if n % c == 0 and c % mult == 0:
            return c
    raise ValueError(f"no tile for {n} (pref {pref}, mult {mult})")


def _dot(a, b):
    return jnp.dot(a, b, preferred_element_type=F32)


def _dot_nt(a, b):
    return lax.dot_general(a, b, (((1,), (1,)), ((), ())), preferred_element_type=F32)


def _dot_tn(a, b):
    return lax.dot_general(a, b, (((0,), (0,)), ((), ())), preferred_element_type=F32)


def _split3(x):
    hi = x.astype(BF16)
    r = x - hi.astype(F32)
    mid = r.astype(BF16)
    lo = (r - mid.astype(F32)).astype(BF16)
    return hi, mid, lo


def _rms(x, w):
    ms = jnp.mean(x * x, axis=-1, keepdims=True)
    return x * lax.rsqrt(ms + EPS) * w


def _silu(x):
    return x * jax.nn.sigmoid(x)


def _rms_cast_kernel(x_ref, w_ref, o_ref):
    o_ref[...] = _rms(x_ref[...], w_ref[...]).astype(o_ref.dtype)


def rms_cast(x, w, out_dtype=BF16):
    m, d = x.shape
    tm = _pick(m, 512, 16)
    return pl.pallas_call(
        _rms_cast_kernel,
        grid=(m // tm,),
        in_specs=[pl.BlockSpec((tm, d), lambda i: (i, 0)), pl.BlockSpec((1, d), lambda i: (0, 0))],
        out_specs=pl.BlockSpec((tm, d), lambda i: (i, 0)),
        out_shape=jax.ShapeDtypeStruct((m, d), out_dtype),
        compiler_params=_params(1),
    )(x, w.reshape(1, d))


def _mm_kernel(x_ref, w_ref, o_ref):
    o_ref[...] = _dot(x_ref[...], w_ref[...]).astype(o_ref.dtype)


def matmul(x, w, tm_pref=512, tn_pref=1024):
    m, k = x.shape
    n = w.shape[1]
    tm = _pick(m, tm_pref, 16)
    tn = _pick(n, tn_pref, LANES)
    return pl.pallas_call(
        _mm_kernel,
        grid=(n // tn, m // tm),
        in_specs=[pl.BlockSpec((tm, k), lambda j, i: (i, 0)), pl.BlockSpec((k, tn), lambda j, i: (0, j))],
        out_specs=pl.BlockSpec((tm, tn), lambda j, i: (i, j)),
        out_shape=jax.ShapeDtypeStruct((m, n), F32),
        compiler_params=_params(2),
    )(x, w)


def _pool_kernel(u_ref, st_ref, pw_ref, sc_ref, y_ref, ext_ref, *, nb, tt, pos0, gc, rc, n_steps):
    n = pl.program_id(1)
    dp = gc * len(POOL_WINDOWS)
    buf = POOL_HALO - 1

    @pl.when(n == 0)
    def _():
        ext_ref[:, 0:1, :] = jnp.zeros((nb, 1, dp), F32)
        ext_ref[:, 1:POOL_HALO, :] = st_ref[...]

    ext_ref[:, POOL_HALO:POOL_HALO + tt, :] = u_ref[...].reshape(nb, tt, dp)
    for c0 in range(0, tt, rc):
        row = lax.broadcasted_iota(jnp.int32, (1, rc, 1), 1) + (pos0 + c0) + n * tt
        for gi, w in enumerate(POOL_WINDOWS):
            lanes = slice(gi * gc, (gi + 1) * gc)
            base = POOL_HALO + c0
            cur = ext_ref[:, base:base + rc, lanes]
            acc = cur
            for i in range(1, w):
                acc = acc + ext_ref[:, base - i:base - i + rc, lanes]
            cnt = jnp.minimum(row + 1, w).astype(F32)
            d = acc / cnt - cur
            y = _dot(d.reshape(nb * rc, gc).astype(BF16), pw_ref[gi]) * sc_ref[:, lanes]
            if nb == 1:
                y_ref[c0:c0 + rc, lanes] = y
            else:
                y_ref[:, lanes] = y
    if n_steps > 1:
        ext_ref[:, 0:POOL_HALO, :] = ext_ref[:, tt:tt + POOL_HALO, :]
    del buf


def pool_mixer(proj, row0, b, t, state, pool_w, pool_scale, pos0):
    dp = state.shape[-1]
    gc = dp // len(POOL_WINDOWS)
    if t >= 512:
        nb, tt = 1, 512
    else:
        nb, tt = _pick(b, max(1, 256 // t), 1), t
    assert t % tt == 0 and b % nb == 0 and row0 % (nb * tt) == 0 and tt % 8 == 0
    assert nb == 1 or tt == t
    rc = min(tt, 256) if nb == 1 else tt
    n_steps = t // tt
    rb0 = row0 // (nb * tt)
    kern = functools.partial(_pool_kernel, nb=nb, tt=tt, pos0=pos0, gc=gc, rc=rc, n_steps=n_steps)
    return pl.pallas_call(
        kern,
        grid=(b // nb, n_steps),
        in_specs=[
            pl.BlockSpec((nb * tt, dp), lambda i, n: (rb0 + i * n_steps + n, 0)),
            pl.BlockSpec((nb, POOL_HALO - 1, dp), lambda i, n: (i, 0, 0)),
            pl.BlockSpec((len(POOL_WINDOWS), gc, gc), lambda i, n: (0, 0, 0)),
            pl.BlockSpec((1, dp), lambda i, n: (0, 0)),
        ],
        out_specs=pl.BlockSpec((nb * tt, dp), lambda i, n: (i * n_steps + n, 0)),
        out_shape=jax.ShapeDtypeStruct((b * t, dp), F32),
        scratch_shapes=[pltpu.VMEM((nb, tt + POOL_HALO, dp), F32)],
        compiler_params=_params(2),
    )(proj, state, pool_w, pool_scale.reshape(1, dp))


def _delta_kernel(q_ref, k_ref, v_ref, z_ref, ab_ref, cs_ref, rs_ref, cw_ref, al_ref, dtb_ref, on_ref,
                  y_ref, so_ref, ext_ref, s_ref, *, nseq, tc, nh, hk, hv, n_steps, taps):
    n = pl.program_id(1)
    c = nseq * tc
    dq = nh * hk
    halo0 = CONV_HALO - (taps - 1)

    @pl.when(n == 0)
    def _():
        ext_ref[:, halo0:CONV_HALO, :] = cs_ref[...]
        for h in range(nh):
            for b in range(nseq):
                s_ref[h, b * hk:(b + 1) * hk, :] = rs_ref[b, h]

    ext_ref[:, CONV_HALO:CONV_HALO + tc, 0:dq] = q_ref[...].reshape(nseq, tc, dq)
    ext_ref[:, CONV_HALO:CONV_HALO + tc, dq:2 * dq] = k_ref[...].reshape(nseq, tc, dq)
    ext_ref[:, CONV_HALO:CONV_HALO + tc, 2 * dq:2 * dq + nh * hv] = v_ref[...].reshape(nseq, tc, nh * hv)

    def conv(col0, width):
        acc = ext_ref[:, halo0:halo0 + tc, col0:col0 + width] * cw_ref[0:1, col0:col0 + width]
        for j in range(1, taps):
            acc = acc + ext_ref[:, halo0 + j:halo0 + j + tc, col0:col0 + width] * cw_ref[j:j + 1, col0:col0 + width]
        return _silu(acc).reshape(c, width)

    ri = lax.broadcasted_iota(jnp.int32, (c, c), 0)
    ci = lax.broadcasted_iota(jnp.int32, (c, c), 1)
    same = (ri // tc) == (ci // tc)
    low = same & (ci <= ri)
    strict = same & (ci < ri)
    eye = (ri == ci).astype(F32)

    ab = ab_ref[...]
    g_full = -jnp.exp(al_ref[...]) * jax.nn.softplus(ab + dtb_ref[...])
    beta_full = jax.nn.sigmoid(ab)
    sel = jnp.concatenate([low.astype(BF16), same.astype(BF16)], axis=0)
    parts = _split3(g_full)
    sums = _dot(sel, parts[0]) + _dot(sel, parts[1]) + _dot(sel, parts[2])
    gcum_full, glast_full = sums[:c], sums[c:]
    pick = (lax.broadcasted_iota(jnp.int32, (8, LANES), 0) == lax.broadcasted_iota(jnp.int32, (8, LANES), 1)).astype(BF16)
    cparts = _split3(gcum_full)
    gcum_t = _dot_nt(pick, cparts[0]) + _dot_nt(pick, cparts[1]) + _dot_nt(pick, cparts[2])

    if nseq > 1:
        rseq = lax.broadcasted_iota(jnp.int32, (c, 1), 0) // tc
        rseq2 = lax.broadcasted_iota(jnp.int32, (2 * c, 1), 0) % c // tc

    def expand(x, seq_of_row):
        if nseq == 1:
            return x
        return jnp.concatenate([jnp.where(seq_of_row == b, x, 0.0) for b in range(nseq)], axis=1)

    levels = int(np.log2(tc))
    assert 2 ** levels == tc
    for h in range(nh):
        gc_ = gcum_full[:, h:h + 1]
        gl_ = glast_full[:, h:h + 1]
        beta = beta_full[:, nh + h:nh + h + 1]
        gd = gc_ - gcum_t[h:h + 1, :]
        decay = jnp.where(low, jnp.exp(jnp.where(low, gd, 0.0)), 0.0)
        qh = conv(h * hk, hk)
        kh = conv(dq + h * hk, hk)
        vh = conv(2 * dq + h * hv, hv)
        qh = qh * lax.rsqrt(jnp.sum(qh * qh, axis=-1, keepdims=True) + EPS) * (hk ** -0.5)
        kh = kh * lax.rsqrt(jnp.sum(kh * kh, axis=-1, keepdims=True) + EPS)
        kb = kh * beta
        kbf = kh.astype(BF16)
        a_kk = jnp.where(strict, _dot_nt(kb.astype(BF16), kbf) * decay, 0.0)
        a_qk = _dot_nt(qh.astype(BF16), kbf) * decay
        x = -a_kk
        p = eye + x
        if levels > 1:
            xb = x.astype(BF16)
            yj = _dot(xb, xb)
            for j in range(1, levels):
                yb = yj.astype(BF16)
                if j < levels - 1:
                    r = _dot(yb, jnp.concatenate([p.astype(BF16), yb], axis=1))
                    p = p + r[:, :c]
                    yj = r[:, c:]
                else:
                    p = p + _dot(yb, p.astype(BF16))
        rhs = jnp.concatenate([vh * beta, kb * jnp.exp(gc_)], axis=1).astype(BF16)
        uw = _dot(p.astype(BF16), rhs)
        u, w = uw[:, :hv], uw[:, hv:]
        s = s_ref[h]
        lhs = jnp.concatenate([w, qh * jnp.exp(gc_)], axis=0)
        ws = _dot(expand(lhs, rseq2 if nseq > 1 else None).astype(BF16), s.astype(BF16))
        v_new = u - ws[:c]
        v_new_b = v_new.astype(BF16)
        o = ws[c:] + _dot(a_qk.astype(BF16), v_new_b)
        kd = kh * jnp.exp(gl_ - gc_)
        upd = _dot_tn(expand(kd, rseq if nseq > 1 else None).astype(BF16), v_new_b)
        if nseq == 1:
            sdec = jnp.exp(gl_[0:1, :])
        else:
            sdec = jnp.concatenate(
                [jnp.broadcast_to(jnp.exp(gl_[b * tc:b * tc + 1, :]), (hk, 1)) for b in range(nseq)], axis=0)
        s_ref[h] = s * sdec + upd
        zh = z_ref[:, h * hv:(h + 1) * hv]
        y_ref[:, h * hv:(h + 1) * hv] = _rms(o, on_ref[...]) * _silu(zh)

    if n_steps > 1:
        ext_ref[:, halo0:CONV_HALO, :] = ext_ref[:, halo0 + tc:CONV_HALO + tc, :]

    @pl.when(n == n_steps - 1)
    def _():
        for h in range(nh):
            for b in range(nseq):
                so_ref[b, h] = s_ref[h, b * hk:(b + 1) * hk, :]


def delta_mixer(proj, ab, row0, b, t, col_q, conv_state, rec_state, conv_w, a_log, dt_bias, o_norm):
    _, nh, hk, hv = rec_state.shape
    dq, dv = nh * hk, nh * hv
    taps = conv_w.shape[0]
    assert dq == dv and col_q % dq == 0 and nh <= 8 and taps - 1 <= CONV_HALO
    if t >= CHUNK:
        nseq, tc = 1, CHUNK
    else:
        tc = t
        nseq = CHUNK // tc
    assert t % tc == 0 and b % nseq == 0 and tc % 8 == 0
    c = nseq * tc
    n_steps = t // tc
    assert row0 % c == 0
    rb0 = row0 // c
    cq = col_q // dq
    pad = lambda v: jnp.zeros((1, LANES), F32).at[0, :nh].set(v.astype(F32))
    kern = functools.partial(_delta_kernel, nseq=nseq, tc=tc, nh=nh, hk=hk, hv=hv, n_steps=n_steps, taps=taps)
    row_map = lambda col: (lambda i, n: (rb0 + i * n_steps + n, col))
    y, s_out = pl.pallas_call(
        kern,
        grid=(b // nseq, n_steps),
        in_specs=[
            pl.BlockSpec((c, dq), row_map(cq)),
            pl.BlockSpec((c, dq), row_map(cq + 1)),
            pl.BlockSpec((c, dv), row_map(cq + 2)),
            pl.BlockSpec((c, dv), row_map(cq + 3)),
            pl.BlockSpec((c, LANES), row_map(0)),
            pl.BlockSpec((nseq, taps - 1, 2 * dq + dv), lambda i, n: (i, 0, 0)),
            pl.BlockSpec((nseq, nh, hk, hv), lambda i, n: (i, 0, 0, 0)),
            pl.BlockSpec((taps, 2 * dq + dv), lambda i, n: (0, 0)),
            pl.BlockSpec((1, LANES), lambda i, n: (0, 0)),
            pl.BlockSpec((1, LANES), lambda i, n: (0, 0)),
            pl.BlockSpec((1, hv), lambda i, n: (0, 0)),
        ],
        out_specs=[
            pl.BlockSpec((c, dv), lambda i, n: (i * n_steps + n, 0)),
            pl.BlockSpec((nseq, nh, hk, hv), lambda i, n: (i, 0, 0, 0)),
        ],
        out_shape=[jax.ShapeDtypeStruct((b * t, dv), F32), jax.ShapeDtypeStruct(rec_state.shape, F32)],
        scratch_shapes=[
            pltpu.VMEM((nseq, tc + CONV_HALO, 2 * dq + dv), F32),
            pltpu.VMEM((nh, nseq * hk, hv), F32),
        ],
        compiler_params=_params(2),
    )(proj, proj, proj, proj, ab, conv_state, rec_state, conv_w, pad(a_log), pad(dt_bias), o_norm.reshape(1, hv))
    return y, s_out


def _merge_kernel(ya_ref, yb_ref, wa_ref, wb_ref, ga_ref, gb_ref, o_ref):
    a = _dot(ya_ref[...].astype(BF16), wa_ref[...])
    b = _dot(yb_ref[...].astype(BF16), wb_ref[...])
    o_ref[...] = (jax.nn.sigmoid(ga_ref[...]) * a + jax.nn.sigmoid(gb_ref[...]) * b).astype(o_ref.dtype)


def merge_branches(y_a, y_b, w_a, w_b, proj, col_gate):
    m, dp = y_a.shape
    dv = y_b.shape[1]
    d = w_a.shape[1]
    tm = _pick(m, 512, 16)
    tn = _pick(d, 1024, LANES)
    assert col_gate % tn == 0
    g0 = col_gate // tn
    nj = d // tn
    return pl.pallas_call(
        _merge_kernel,
        grid=(nj, m // tm),
        in_specs=[
            pl.BlockSpec((tm, dp), lambda j, i: (i, 0)),
            pl.BlockSpec((tm, dv), lambda j, i: (i, 0)),
            pl.BlockSpec((dp, tn), lambda j, i: (0, j)),
            pl.BlockSpec((dv, tn), lambda j, i: (0, j)),
            pl.BlockSpec((tm, tn), lambda j, i: (i, g0 + j)),
            pl.BlockSpec((tm, tn), lambda j, i: (i, g0 + nj + j)),
        ],
        out_specs=pl.BlockSpec((tm, tn), lambda j, i: (i, j)),
        out_shape=jax.ShapeDtypeStruct((m, d), BF16),
        compiler_params=_params(2),
    )(y_a, y_b, w_a, w_b, proj, proj)


def _outproj_kernel(m_ref, w_ref, h_ref, nw_ref, ho_ref, f_ref):
    hn = h_ref[...] + _dot(m_ref[...], w_ref[...])
    ho_ref[...] = hn
    f_ref[...] = _rms(hn, nw_ref[...]).astype(f_ref.dtype)


def out_proj(merged, w_out, h, norm_w):
    m, d = h.shape
    tm = _pick(m, 256, 16)
    return pl.pallas_call(
        _outproj_kernel,
        grid=(m // tm,),
        in_specs=[
            pl.BlockSpec((tm, d), lambda i: (i, 0)),
            pl.BlockSpec((d, d), lambda i: (0, 0)),
            pl.BlockSpec((tm, d), lambda i: (i, 0)),
            pl.BlockSpec((1, d), lambda i: (0, 0)),
        ],
        out_specs=[pl.BlockSpec((tm, d), lambda i: (i, 0)), pl.BlockSpec((tm, d), lambda i: (i, 0))],
        out_shape=[jax.ShapeDtypeStruct((m, d), F32), jax.ShapeDtypeStruct((m, d), BF16)],
        compiler_params=_params(1),
    )(merged, w_out, h, norm_w.reshape(1, d))


def _mlp_kernel(se_ref, ns_ref, x_ref, wg_ref, wu_ref, wd_ref, *rest, n_sub, cast, residual):
    if residual:
        res_ref, o_ref, *scratch = rest
    else:
        res_ref = None
        o_ref, *scratch = rest
    s = pl.program_id(0)
    j = pl.program_id(1)
    n = ns_ref[s]
    if cast:
        wgb, wub, wdb = scratch

        @pl.when(n > 0)
        def _():
            wgb[...] = wg_ref[...].astype(BF16)
            wub[...] = wu_ref[...].astype(BF16)
            wdb[...] = wd_ref[...].astype(BF16)
    else:
        wgb, wub, wdb = wg_ref, wu_ref, wd_ref

    def rows_of(r):
        return pl.ds(pl.multiple_of(r * SUB_ROWS, SUB_ROWS), SUB_ROWS)

    def partial_out(r):
        x = x_ref[rows_of(r), :]
        g = _dot(x, wgb[...])
        u = _dot(x, wub[...])
        return _dot((_silu(g) * u).astype(BF16), wdb[...])

    @pl.when(j == 0)
    def _():
        def first(r, carry):
            y = partial_out(r)
            if residual:
                y = y + res_ref[rows_of(r), :]
            o_ref[rows_of(r), :] = y
            return carry

        lax.fori_loop(0, n, first, 0)

        def blank(r, carry):
            o_ref[rows_of(r), :] = jnp.zeros((SUB_ROWS, o_ref.shape[1]), F32)
            return carry

        lax.fori_loop(n, n_sub, blank, 0)

    @pl.when(j > 0)
    def _():
        def later(r, carry):
            o_ref[rows_of(r), :] += partial_out(r)
            return carry

        lax.fori_loop(0, n, later, 0)


def fused_mlp(x, w_gate, w_up, w_down, tile_expert, tile_sub, tm, tf, residual=None):
    r, d = x.shape
    e, _, f = w_gate.shape
    n_tiles = r // tm
    n_sub = tm // SUB_ROWS
    nf = f // tf
    cast = w_gate.dtype != BF16
    assert r % tm == 0 and tm % SUB_ROWS == 0 and f % tf == 0

    def wcol(s, j, se, ns):
        return (se[s], 0, jnp.where(ns[s] > 0, j, nf - 1))

    def wrow(s, j, se, ns):
        return (se[s], jnp.where(ns[s] > 0, j, nf - 1), 0)

    in_specs = [
        pl.BlockSpec((tm, d), lambda s, j, se, ns: (s, 0)),
        pl.BlockSpec((None, d, tf), wcol),
        pl.BlockSpec((None, d, tf), wcol),
        pl.BlockSpec((None, tf, d), wrow),
    ]
    args = [x, w_gate, w_up, w_down]
    if residual is not None:
        in_specs.append(pl.BlockSpec((tm, d), lambda s, j, se, ns: (s, 0)))
        args.append(residual)
    scratch = [pltpu.VMEM((d, tf), BF16), pltpu.VMEM((d, tf), BF16), pltpu.VMEM((tf, d), BF16)] if cast else []
    kern = functools.partial(_mlp_kernel, n_sub=n_sub, cast=cast, residual=residual is not None)
    return pl.pallas_call(
        kern,
        grid_spec=pltpu.PrefetchScalarGridSpec(
            num_scalar_prefetch=2,
            grid=(n_tiles, nf),
            in_specs=in_specs,
            out_specs=pl.BlockSpec((tm, d), lambda s, j, se, ns: (s, 0)),
            scratch_shapes=scratch,
        ),
        out_shape=jax.ShapeDtypeStruct((r, d), F32),
        compiler_params=_params(2),
    )(tile_expert, tile_sub, *args)


def _router_kernel(h_ref, nw_ref, r_ref, f_ref, idx_ref, gate_ref, *, n_exp):
    f = _rms(h_ref[...], nw_ref[...])
    f_ref[...] = f.astype(f_ref.dtype)
    f_hi = f.astype(BF16)
    f_lo = (f - f_hi.astype(F32)).astype(BF16)
    r = r_ref[...]
    r_hi = r.astype(BF16)
    r_lo = (r - r_hi.astype(F32)).astype(BF16)
    logits = _dot(f_hi, r_hi) + (_dot(f_lo, r_hi) + _dot(f_hi, r_lo))
    lane = lax.broadcasted_iota(jnp.int32, logits.shape, 1)
    m1 = jnp.max(logits, axis=-1, keepdims=True)
    i1 = jnp.min(jnp.where(logits == m1, lane, n_exp), axis=-1, keepdims=True)
    rest = jnp.where(lane == i1, -jnp.inf, logits)
    m2 = jnp.max(rest, axis=-1, keepdims=True)
    i2 = jnp.min(jnp.where(rest == m2, lane, n_exp), axis=-1, keepdims=True)
    ex = jnp.exp(m2 - m1)
    den = 1.0 + ex
    two = lax.broadcasted_iota(jnp.int32, idx_ref.shape, 1)
    idx_ref[...] = jnp.where(two == 0, i1, i2)
    gate_ref[...] = jnp.where(two == 0, 1.0 / den, ex / den)


def router_top2(h, norm_w, router_w):
    m, d = h.shape
    n_exp = router_w.shape[1]
    tm = _pick(m, 512, 16)
    kern = functools.partial(_router_kernel, n_exp=n_exp)
    return pl.pallas_call(
        kern,
        grid=(m // tm,),
        in_specs=[
            pl.BlockSpec((tm, d), lambda i: (i, 0)),
            pl.BlockSpec((1, d), lambda i: (0, 0)),
            pl.BlockSpec((d, n_exp), lambda i: (0, 0)),
        ],
        out_specs=[
            pl.BlockSpec((tm, d), lambda i: (i, 0)),
            pl.BlockSpec((tm, TOP_K), lambda i: (i, 0)),
            pl.BlockSpec((tm, TOP_K), lambda i: (i, 0)),
        ],
        out_shape=[
            jax.ShapeDtypeStruct((m, d), BF16),
            jax.ShapeDtypeStruct((m, TOP_K), jnp.int32),
            jax.ShapeDtypeStruct((m, TOP_K), F32),
        ],
        compiler_params=_params(1),
    )(h, norm_w.reshape(1, d), router_w)


def _gather_kernel(idx_ref, src_ref, dst_ref, sem, *, ch):
    base = pl.program_id(0) * ch

    def row_copy(r):
        return pltpu.make_async_copy(src_ref.at[idx_ref[r]], dst_ref.at[base + r], sem)

    def start(r, carry):
        row_copy(r).start()
        return carry

    def wait(r, carry):
        row_copy(r).wait()
        return carry

    lax.fori_loop(0, ch, start, 0)
    lax.fori_loop(0, ch, wait, 0)


def gather_rows(src, idx):
    m, d = src.shape
    r = idx.shape[0]
    ch = _pick(r, 1024, 8)
    src3 = src.reshape(m, d // LANES, LANES)
    kern = functools.partial(_gather_kernel, ch=ch)
    out = pl.pallas_call(
        kern,
        grid=(r // ch,),
        in_specs=[
            pl.BlockSpec((ch,), lambda i: (i,), memory_space=pltpu.SMEM),
            pl.BlockSpec(memory_space=pl.ANY),
        ],
        out_specs=pl.BlockSpec(memory_space=pl.ANY),
        out_shape=jax.ShapeDtypeStruct((r, d // LANES, LANES), src.dtype),
        scratch_shapes=[pltpu.SemaphoreType.DMA(())],
        compiler_params=_params(1),
    )(idx, src3)
    return out.reshape(r, d)


def _combine_kernel(pos_ref, o_ref, g_ref, h_ref, nw_ref, y_ref, buf, sem, *, tc):
    def row_copy(t, k):
        return pltpu.make_async_copy(o_ref.at[pl.ds(pos_ref[TOP_K * t + k], 1)], buf.at[k, pl.ds(t, 1)], sem)

    def start(t, carry):
        for k in range(TOP_K):
            row_copy(t, k).start()
        return carry

    def wait(t, carry):
        for k in range(TOP_K):
            row_copy(t, k).wait()
        return carry

    lax.fori_loop(0, tc, start, 0)
    lax.fori_loop(0, tc, wait, 0)
    g = g_ref[...]
    acc = h_ref[...]
    for k in range(TOP_K):
        acc = acc + g[:, k:k + 1] * buf[k]
    y_ref[...] = _rms(acc, nw_ref[...])


def combine_experts(expert_out, pos, gates, h, norm_w):
    m, d = h.shape
    tc = _pick(m, 256, 8)
    kern = functools.partial(_combine_kernel, tc=tc)
    return pl.pallas_call(
        kern,
        grid=(m // tc,),
        in_specs=[
            pl.BlockSpec((TOP_K * tc,), lambda i: (i,), memory_space=pltpu.SMEM),
            pl.BlockSpec(memory_space=pl.ANY),
            pl.BlockSpec((tc, TOP_K), lambda i: (i, 0)),
            pl.BlockSpec((tc, d), lambda i: (i, 0)),
            pl.BlockSpec((1, d), lambda i: (0, 0)),
        ],
        out_specs=pl.BlockSpec((tc, d), lambda i: (i, 0)),
        out_shape=jax.ShapeDtypeStruct((m, d), F32),
        scratch_shapes=[pltpu.VMEM((TOP_K, tc, d), F32), pltpu.SemaphoreType.DMA(())],
        compiler_params=_params(1),
    )(pos.reshape(-1), expert_out, gates, h, norm_w.reshape(1, d))


def routing_tables(top_i, n_exp, tm):
    m = top_i.shape[0]
    n_pairs = m * TOP_K
    n_tiles = n_pairs // tm + n_exp
    e_flat = top_i.reshape(-1)
    onehot = (e_flat[:, None] == jnp.arange(n_exp, dtype=jnp.int32)[None, :]).astype(jnp.int32)
    incl = jnp.cumsum(onehot, axis=0)
    counts = incl[-1]
    rank = jnp.sum((incl - onehot) * onehot, axis=1)
    tiles_e = (counts + tm - 1) // tm
    tile_end = jnp.cumsum(tiles_e)
    tile_start = tile_end - tiles_e
    dest = (tile_start * tm)[e_flat] + rank
    src_tok = jnp.zeros((n_tiles * tm,), jnp.int32).at[dest].set(jnp.arange(n_pairs, dtype=jnp.int32) // TOP_K)
    tile = jnp.arange(n_tiles, dtype=jnp.int32)
    used = tile < tile_end[-1]
    t_exp = jnp.minimum(jnp.sum((tile[:, None] >= tile_end[None, :]).astype(jnp.int32), axis=1), n_exp - 1)
    last_exp = t_exp[jnp.maximum(tile_end[-1] - 1, 0)]
    t_exp = jnp.where(used, t_exp, last_exp)
    rows_left = counts[t_exp] - (tile - tile_start[t_exp]) * tm
    t_sub = jnp.where(used, (jnp.clip(rows_left, 0, tm) + SUB_ROWS - 1) // SUB_ROWS, 0)
    return src_tok, dest.reshape(m, TOP_K), t_exp.astype(jnp.int32), t_sub.astype(jnp.int32)


def moe_layer(h, norm_w, router_w, w_gate, w_up, w_down, final_norm_w):
    m, d = h.shape
    n_exp = router_w.shape[1]
    f = w_gate.shape[-1]
    tm = 1024 if m * TOP_K >= 8192 else SUB_ROWS
    tf = _pick(f, 256, LANES)
    f_bf, top_i, gates = router_top2(h, norm_w, router_w)
    src_tok, pos, t_exp, t_sub = routing_tables(top_i, n_exp, tm)
    x_sorted = gather_rows(f_bf, src_tok)
    expert_out = fused_mlp(x_sorted, w_gate, w_up, w_down, t_exp, t_sub, tm, tf)
    return combine_experts(expert_out, pos, gates, h, final_norm_w)


def dense_ffn(f_bf, h, w_gate, w_up, w_down):
    m, d = h.shape
    f = w_gate.shape[-1]
    tm = _pick(m, 512, SUB_ROWS)
    tf = _pick(f, 512, LANES)
    n_tiles = m // tm
    t_exp = jnp.zeros((n_tiles,), jnp.int32)
    t_sub = jnp.full((n_tiles,), tm // SUB_ROWS, jnp.int32)
    return fused_mlp(f_bf, w_gate[None], w_up[None], w_down[None], t_exp, t_sub, tm, tf, residual=h)


def kernel(x_prompt, x_sample, state_pool, state_conv, state_rec, norm_mix, w_in, pool_w, pool_scale, conv_w,
           a_log, dt_bias, o_norm, w_branch_a, w_branch_b, w_out, norm_ffn, ffn_w_gate, ffn_w_up, ffn_w_down,
           router, moe_w_gate, moe_w_up, moe_w_down, norm_final):
    bp, tp, d = x_prompt.shape
    bs, ts, _ = x_sample.shape
    depth = w_in.shape[0]
    assert depth == 2, "layer 0 uses the dense FFN, layer 1 the expert FFN followed by the final norm"
    dp = state_pool.shape[-1]
    _, _, nh, hk, hv = state_rec.shape
    dq, dv = nh * hk, nh * hv
    taps = conv_w.shape[1]
    mp, ms = bp * tp, bs * ts
    n_main = dp + 2 * dq + 2 * dv
    col_ab = n_main
    col_gate = n_main + 2 * nh
    assert w_in.shape[-1] == col_gate + 2 * d

    h = jnp.concatenate([x_prompt.reshape(mp, d), x_sample.reshape(ms, d)], axis=0)
    zeros_pool = jnp.zeros((bp,) + state_pool.shape[2:], F32)
    zeros_conv = jnp.zeros((bp,) + state_conv.shape[2:], F32)
    zeros_rec = jnp.zeros((bp,) + state_rec.shape[2:], F32)
    pools_p, convs_p, recs_p, pools_s, convs_s, recs_s = [], [], [], [], [], []
    y = None
    for layer in range(depth):
        w_l = w_in[layer]
        w_main = jnp.concatenate([w_l[:, :n_main], w_l[:, col_gate:]], axis=1).astype(BF16)
        w_ab = jnp.zeros((d, LANES), BF16).at[:, :2 * nh].set(w_l[:, col_ab:col_gate].astype(BF16))
        a = rms_cast(h, norm_mix[layer])
        proj = matmul(a, w_main)
        ab = matmul(a, w_ab)
        pw = pool_w[layer].astype(BF16)
        ya_p = pool_mixer(proj, 0, bp, tp, zeros_pool, pw, pool_scale[layer], 0)
        ya_s = pool_mixer(proj, mp, bs, ts, state_pool[layer], pw, pool_scale[layer], PAST_LEN)
        yb_p, rec_p = delta_mixer(proj, ab, 0, bp, tp, dp, zeros_conv, zeros_rec, conv_w[layer], a_log[layer],
                                  dt_bias[layer], o_norm[layer])
        yb_s, rec_s = delta_mixer(proj, ab, mp, bs, ts, dp, state_conv[layer], state_rec[layer], conv_w[layer],
                                  a_log[layer], dt_bias[layer], o_norm[layer])
        merged = merge_branches(jnp.concatenate([ya_p, ya_s], axis=0), jnp.concatenate([yb_p, yb_s], axis=0),
                                w_branch_a[layer].astype(BF16), w_branch_b[layer].astype(BF16), proj, n_main)
        h, f_bf = out_proj(merged, w_out[layer].astype(BF16), h, norm_ffn[layer])
        if layer == 0:
            h = dense_ffn(f_bf, h, ffn_w_gate[0].astype(BF16), ffn_w_up[0].astype(BF16), ffn_w_down[0].astype(BF16))
        else:
            y = moe_layer(h, norm_ffn[layer], router[0], moe_w_gate[0], moe_w_up[0], moe_w_down[0], norm_final)
        u_p = proj[:mp, :dp].reshape(bp, tp, dp)
        u_s = proj[mp:, :dp].reshape(bs, ts, dp)
        qkv_p = proj[:mp, dp:dp + 2 * dq + dv].reshape(bp, tp, -1)
        qkv_s = proj[mp:, dp:dp + 2 * dq + dv].reshape(bs, ts, -1)
        nbuf = state_pool.shape[2]
        pools_p.append(jnp.concatenate([zeros_pool, u_p], axis=1)[:, -nbuf:])
        pools_s.append(jnp.concatenate([state_pool[layer], u_s], axis=1)[:, -nbuf:])
        convs_p.append(jnp.concatenate([zeros_conv, qkv_p], axis=1)[:, -(taps - 1):])
        convs_s.append(jnp.concatenate([state_conv[layer], qkv_s], axis=1)[:, -(taps - 1):])
        recs_p.append(rec_p)
        recs_s.append(rec_s)
    return (y[:mp].reshape(bp, tp, d), y[mp:].reshape(bs, ts, d),
            jnp.stack(pools_p), jnp.stack(convs_p), jnp.stack(recs_p),
            jnp.stack(pools_s), jnp.stack(convs_s), jnp.stack(recs_s))
```

```python
import functools

import jax
import jax.numpy as jnp
import numpy as np
from jax import lax
from jax.experimental import pallas as pl
from jax.experimental.pallas import tpu as pltpu

F32 = jnp.float32
BF16 = jnp.bfloat16
EPS = 1e-6
POOL_WINDOWS = (2, 4, 8, 16)
POOL_HALO = 16
CONV_HALO = 8
CHUNK = 64
TOP_K = 2
PAST_LEN = 16384
LANES = 128
SUB_ROWS = 256
VMEM_BYTES = 56 * 1024 * 1024


def _params(n_axes, vmem=VMEM_BYTES):
    return pltpu.CompilerParams(dimension_semantics=("arbitrary",) * n_axes, vmem_limit_bytes=vmem)


def _pick(n, pref, mult):
    if n <= pref:
        return n
    for c in range(pref, 0, -1):
        if n % c == 0 and c % mult == 0:
            return c
    raise ValueError(f"no tile for {n} (pref {pref}, mult {mult})")


def _dot(a, b):
    return jnp.dot(a, b, preferred_element_type=F32)


def _dot_nt(a, b):
    return lax.dot_general(a, b, (((1,), (1,)), ((), ())), preferred_element_type=F32)


def _dot_tn(a, b):
    return lax.dot_general(a, b, (((0,), (0,)), ((), ())), preferred_element_type=F32)


def _split3(x):
    hi = x.astype(BF16)
    r = x - hi.astype(F32)
    mid = r.astype(BF16)
    lo = (r - mid.astype(F32)).astype(BF16)
    return hi, mid, lo


def _rms(x, w):
    ms = jnp.mean(x * x, axis=-1, keepdims=True)
    return x * lax.rsqrt(ms + EPS) * w


def _silu(x):
    return x * jax.nn.sigmoid(x)


def _rms_cast_kernel(x_ref, w_ref, o_ref):
    o_ref[...] = _rms(x_ref[...], w_ref[...]).astype(o_ref.dtype)


def rms_cast(x, w, out_dtype=BF16):
    m, d = x.shape
    tm = _pick(m, 512, 16)
    return pl.pallas_call(
        _rms_cast_kernel,
        grid=(m // tm,),
        in_specs=[pl.BlockSpec((tm, d), lambda i: (i, 0)), pl.BlockSpec((1, d), lambda i: (0, 0))],
        out_specs=pl.BlockSpec((tm, d), lambda i: (i, 0)),
        out_shape=jax.ShapeDtypeStruct((m, d), out_dtype),
        compiler_params=_params(1),
        name="rms_cast",
    )(x, w.reshape(1, d))


def _mm_kernel(x_ref, w_ref, o_ref, wb_ref):
    @pl.when(pl.program_id(1) == 0)
    def _():
        wb_ref[...] = w_ref[...].astype(BF16)

    o_ref[...] = _dot(x_ref[...], wb_ref[...])


def matmul(x, w, n, tm_pref=512, tn_pref=1024):
    m, k = x.shape
    tm = _pick(m, tm_pref, 16)
    tn = _pick(n, tn_pref, LANES)
    assert w.shape[1] >= n and w.dtype == F32
    return pl.pallas_call(
        _mm_kernel,
        grid=(n // tn, m // tm),
        in_specs=[pl.BlockSpec((tm, k), lambda j, i: (i, 0)), pl.BlockSpec((k, tn), lambda j, i: (0, j))],
        out_specs=pl.BlockSpec((tm, tn), lambda j, i: (i, j)),
        out_shape=jax.ShapeDtypeStruct((m, n), F32),
        scratch_shapes=[pltpu.VMEM((k, tn), BF16)],
        compiler_params=_params(2),
        name="in_proj",
    )(x, w)


def _pool_kernel(u_ref, st_ref, pw_ref, sc_ref, y_ref, ext_ref, *, nb, tt, pos0, gc, rc, n_steps):
    n = pl.program_id(1)
    dp = gc * len(POOL_WINDOWS)
    buf = POOL_HALO - 1

    @pl.when(n == 0)
    def _():
        ext_ref[:, 0:1, :] = jnp.zeros((nb, 1, dp), F32)
        ext_ref[:, 1:POOL_HALO, :] = st_ref[...]

    ext_ref[:, POOL_HALO:POOL_HALO + tt, :] = u_ref[...].reshape(nb, tt, dp)
    for c0 in range(0, tt, rc):
        row = lax.broadcasted_iota(jnp.int32, (1, rc, 1), 1) + (pos0 + c0) + n * tt
        for gi, w in enumerate(POOL_WINDOWS):
            lanes = slice(gi * gc, (gi + 1) * gc)
            base = POOL_HALO + c0
            cur = ext_ref[:, base:base + rc, lanes]
            acc = cur
            for i in range(1, w):
                acc = acc + ext_ref[:, base - i:base - i + rc, lanes]
            cnt = jnp.minimum(row + 1, w).astype(F32)
            d = acc / cnt - cur
            y = _dot(d.reshape(nb * rc, gc).astype(BF16), pw_ref[gi]) * sc_ref[:, lanes]
            if nb == 1:
                y_ref[c0:c0 + rc, lanes] = y
            else:
                y_ref[:, lanes] = y
    if n_steps > 1:
        ext_ref[:, 0:POOL_HALO, :] = ext_ref[:, tt:tt + POOL_HALO, :]
    del buf


def pool_mixer(proj, row0, b, t, state, pool_w, pool_scale, pos0):
    dp = state.shape[-1]
    gc = dp // len(POOL_WINDOWS)
    if t >= 512:
        nb, tt = 1, 512
    else:
        nb, tt = _pick(b, max(1, 256 // t), 1), t
    assert t % tt == 0 and b % nb == 0 and row0 % (nb * tt) == 0 and tt % 8 == 0
    assert nb == 1 or tt == t
    rc = min(tt, 256) if nb == 1 else tt
    n_steps = t // tt
    rb0 = row0 // (nb * tt)
    kern = functools.partial(_pool_kernel, nb=nb, tt=tt, pos0=pos0, gc=gc, rc=rc, n_steps=n_steps)
    return pl.pallas_call(
        kern,
        grid=(b // nb, n_steps),
        in_specs=[
            pl.BlockSpec((nb * tt, dp), lambda i, n: (rb0 + i * n_steps + n, 0)),
            pl.BlockSpec((nb, POOL_HALO - 1, dp), lambda i, n: (i, 0, 0)),
            pl.BlockSpec((len(POOL_WINDOWS), gc, gc), lambda i, n: (0, 0, 0)),
            pl.BlockSpec((1, dp), lambda i, n: (0, 0)),
        ],
        out_specs=pl.BlockSpec((nb * tt, dp), lambda i, n: (i * n_steps + n, 0)),
        out_shape=jax.ShapeDtypeStruct((b * t, dp), F32),
        scratch_shapes=[pltpu.VMEM((nb, tt + POOL_HALO, dp), F32)],
        compiler_params=_params(2),
        name="pool_mixer",
    )(proj, state, pool_w, pool_scale.reshape(1, dp))


def _delta_kernel(q_ref, k_ref, v_ref, z_ref, ab_ref, cs_ref, rs_ref, cw_ref, al_ref, dtb_ref, on_ref,
                  y_ref, so_ref, ext_ref, s_ref, *, nseq, tc, nh, hk, hv, n_steps, taps):
    n = pl.program_id(1)
    c = nseq * tc
    dq = nh * hk
    halo0 = CONV_HALO - (taps - 1)

    @pl.when(n == 0)
    def _():
        ext_ref[:, halo0:CONV_HALO, :] = cs_ref[...]
        for h in range(nh):
            for b in range(nseq):
                s_ref[h, b * hk:(b + 1) * hk, :] = rs_ref[b, h]

    ext_ref[:, CONV_HALO:CONV_HALO + tc, 0:dq] = q_ref[...].reshape(nseq, tc, dq)
    ext_ref[:, CONV_HALO:CONV_HALO + tc, dq:2 * dq] = k_ref[...].reshape(nseq, tc, dq)
    ext_ref[:, CONV_HALO:CONV_HALO + tc, 2 * dq:2 * dq + nh * hv] = v_ref[...].reshape(nseq, tc, nh * hv)

    def conv(col0, width):
        acc = ext_ref[:, halo0:halo0 + tc, col0:col0 + width] * cw_ref[0:1, col0:col0 + width]
        for j in range(1, taps):
            acc = acc + ext_ref[:, halo0 + j:halo0 + j + tc, col0:col0 + width] * cw_ref[j:j + 1, col0:col0 + width]
        return _silu(acc).reshape(c, width)

    ri = lax.broadcasted_iota(jnp.int32, (c, c), 0)
    ci = lax.broadcasted_iota(jnp.int32, (c, c), 1)
    same = (ri // tc) == (ci // tc)
    low = same & (ci <= ri)
    strict = same & (ci < ri)
    eye = (ri == ci).astype(F32)

    ab = ab_ref[...]
    g_full = -jnp.exp(al_ref[...]) * jax.nn.softplus(ab + dtb_ref[...])
    beta_full = jax.nn.sigmoid(ab)
    sel = jnp.concatenate([low.astype(BF16), same.astype(BF16)], axis=0)
    parts = _split3(g_full)
    sums = _dot(sel, parts[0]) + _dot(sel, parts[1]) + _dot(sel, parts[2])
    gcum_full, glast_full = sums[:c], sums[c:]
    pick = (lax.broadcasted_iota(jnp.int32, (8, LANES), 0) == lax.broadcasted_iota(jnp.int32, (8, LANES), 1)).astype(BF16)
    cparts = _split3(gcum_full)
    gcum_t = _dot_nt(pick, cparts[0]) + _dot_nt(pick, cparts[1]) + _dot_nt(pick, cparts[2])

    if nseq > 1:
        rseq = lax.broadcasted_iota(jnp.int32, (c, 1), 0) // tc
        rseq2 = lax.broadcasted_iota(jnp.int32, (2 * c, 1), 0) % c // tc

    def expand(x, seq_of_row):
        if nseq == 1:
            return x
        return jnp.concatenate([jnp.where(seq_of_row == b, x, 0.0) for b in range(nseq)], axis=1)

    levels = int(np.log2(tc))
    assert 2 ** levels == tc
    heads = range(nh)
    s_in = [s_ref[h] for h in heads]
    z_in = [z_ref[:, h * hv:(h + 1) * hv] for h in heads]
    q_c = [conv(h * hk, hk) for h in heads]
    k_c = [conv(dq + h * hk, hk) for h in heads]
    v_c = [conv(2 * dq + h * hv, hv) for h in heads]
    gcs = [gcum_full[:, h:h + 1] for h in heads]
    gls = [glast_full[:, h:h + 1] for h in heads]
    betas = [beta_full[:, nh + h:nh + h + 1] for h in heads]
    decays = [jnp.where(low, jnp.exp(jnp.where(low, gcs[h] - gcum_t[h:h + 1, :], 0.0)), 0.0) for h in heads]
    qs = [q_c[h] * lax.rsqrt(jnp.sum(q_c[h] * q_c[h], axis=-1, keepdims=True) + EPS) * (hk ** -0.5) for h in heads]
    ks = [k_c[h] * lax.rsqrt(jnp.sum(k_c[h] * k_c[h], axis=-1, keepdims=True) + EPS) for h in heads]
    kbs = [ks[h] * betas[h] for h in heads]
    kbf = [ks[h].astype(BF16) for h in heads]
    a_kk = [jnp.where(strict, _dot_nt(kbs[h].astype(BF16), kbf[h]) * decays[h], 0.0) for h in heads]
    a_qk = [(_dot_nt(qs[h].astype(BF16), kbf[h]) * decays[h]).astype(BF16) for h in heads]
    ps = [eye - a_kk[h] for h in heads]
    if levels > 1:
        xb = [(-a_kk[h]).astype(BF16) for h in heads]
        yj = [_dot(xb[h], xb[h]) for h in heads]
        for j in range(1, levels):
            yb = [yj[h].astype(BF16) for h in heads]
            if j < levels - 1:
                r = [_dot(yb[h], jnp.concatenate([ps[h].astype(BF16), yb[h]], axis=1)) for h in heads]
                ps = [ps[h] + r[h][:, :c] for h in heads]
                yj = [r[h][:, c:] for h in heads]
            else:
                ps = [ps[h] + _dot(yb[h], ps[h].astype(BF16)) for h in heads]
    egc = [jnp.exp(gcs[h]) for h in heads]
    uw = [_dot(ps[h].astype(BF16), jnp.concatenate([v_c[h] * betas[h], kbs[h] * egc[h]], axis=1).astype(BF16))
          for h in heads]
    lhs = [expand(jnp.concatenate([uw[h][:, hv:], qs[h] * egc[h]], axis=0), rseq2 if nseq > 1 else None).astype(BF16)
           for h in heads]
    ws = [_dot(lhs[h], s_in[h].astype(BF16)) for h in heads]
    v_new = [(uw[h][:, :hv] - ws[h][:c]).astype(BF16) for h in heads]
    o = [ws[h][c:] + _dot(a_qk[h], v_new[h]) for h in heads]
    kd = [expand(ks[h] * jnp.exp(gls[h] - gcs[h]), rseq if nseq > 1 else None).astype(BF16) for h in heads]
    upd = [_dot_tn(kd[h], v_new[h]) for h in heads]
    if nseq == 1:
        sdec = [jnp.exp(gls[h][0:1, :]) for h in heads]
    else:
        sdec = [jnp.concatenate([jnp.broadcast_to(jnp.exp(gls[h][b * tc:b * tc + 1, :]), (hk, 1))
                                 for b in range(nseq)], axis=0) for h in heads]
    s_out = [s_in[h] * sdec[h] + upd[h] for h in heads]
    y_out = [_rms(o[h], on_ref[...]) * _silu(z_in[h]) for h in heads]
    for h in heads:
        s_ref[h] = s_out[h]
        y_ref[:, h * hv:(h + 1) * hv] = y_out[h]

    if n_steps > 1:
        ext_ref[:, halo0:CONV_HALO, :] = ext_ref[:, halo0 + tc:CONV_HALO + tc, :]

    @pl.when(n == n_steps - 1)
    def _():
        for h in range(nh):
            for b in range(nseq):
                so_ref[b, h] = s_ref[h, b * hk:(b + 1) * hk, :]


def delta_mixer(proj, ab, row0, b, t, col_q, conv_state, rec_state, conv_w, a_log, dt_bias, o_norm):
    _, nh, hk, hv = rec_state.shape
    dq, dv = nh * hk, nh * hv
    taps = conv_w.shape[0]
    assert dq == dv and col_q % dq == 0 and nh <= 8 and taps - 1 <= CONV_HALO
    if t >= CHUNK:
        nseq, tc = 1, CHUNK
    else:
        tc = t
        nseq = CHUNK // tc
    assert t % tc == 0 and b % nseq == 0 and tc % 8 == 0
    c = nseq * tc
    n_steps = t // tc
    assert row0 % c == 0
    rb0 = row0 // c
    cq = col_q // dq
    pad = lambda v: jnp.zeros((1, LANES), F32).at[0, :nh].set(v.astype(F32))
    kern = functools.partial(_delta_kernel, nseq=nseq, tc=tc, nh=nh, hk=hk, hv=hv, n_steps=n_steps, taps=taps)
    row_map = lambda col: (lambda i, n: (rb0 + i * n_steps + n, col))
    y, s_out = pl.pallas_call(
        kern,
        grid=(b // nseq, n_steps),
        in_specs=[
            pl.BlockSpec((c, dq), row_map(cq)),
            pl.BlockSpec((c, dq), row_map(cq + 1)),
            pl.BlockSpec((c, dv), row_map(cq + 2)),
            pl.BlockSpec((c, dv), row_map(cq + 3)),
            pl.BlockSpec((c, LANES), row_map(0)),
            pl.BlockSpec((nseq, taps - 1, 2 * dq + dv), lambda i, n: (i, 0, 0)),
            pl.BlockSpec((nseq, nh, hk, hv), lambda i, n: (i, 0, 0, 0)),
            pl.BlockSpec((taps, 2 * dq + dv), lambda i, n: (0, 0)),
            pl.BlockSpec((1, LANES), lambda i, n: (0, 0)),
            pl.BlockSpec((1, LANES), lambda i, n: (0, 0)),
            pl.BlockSpec((1, hv), lambda i, n: (0, 0)),
        ],
        out_specs=[
            pl.BlockSpec((c, dv), lambda i, n: (i * n_steps + n, 0)),
            pl.BlockSpec((nseq, nh, hk, hv), lambda i, n: (i, 0, 0, 0)),
        ],
        out_shape=[jax.ShapeDtypeStruct((b * t, dv), F32), jax.ShapeDtypeStruct(rec_state.shape, F32)],
        scratch_shapes=[
            pltpu.VMEM((nseq, tc + CONV_HALO, 2 * dq + dv), F32),
            pltpu.VMEM((nh, nseq * hk, hv), F32),
        ],
        compiler_params=_params(2),
        name="delta_mixer",
    )(proj, proj, proj, proj, ab, conv_state, rec_state, conv_w, pad(a_log), pad(dt_bias), o_norm.reshape(1, hv))
    return y, s_out


def _merge_kernel(ya_ref, yb_ref, wa_ref, wb_ref, ga_ref, gb_ref, o_ref):
    a = _dot(ya_ref[...].astype(BF16), wa_ref[...])
    b = _dot(yb_ref[...].astype(BF16), wb_ref[...])
    o_ref[...] = (jax.nn.sigmoid(ga_ref[...]) * a + jax.nn.sigmoid(gb_ref[...]) * b).astype(o_ref.dtype)


def merge_branches(y_a, y_b, w_a, w_b, proj, col_gate):
    m, dp = y_a.shape
    dv = y_b.shape[1]
    d = w_a.shape[1]
    tm = _pick(m, 512, 16)
    tn = _pick(d, 1024, LANES)
    assert col_gate % tn == 0
    g0 = col_gate // tn
    nj = d // tn
    return pl.pallas_call(
        _merge_kernel,
        grid=(nj, m // tm),
        in_specs=[
            pl.BlockSpec((tm, dp), lambda j, i: (i, 0)),
            pl.BlockSpec((tm, dv), lambda j, i: (i, 0)),
            pl.BlockSpec((dp, tn), lambda j, i: (0, j)),
            pl.BlockSpec((dv, tn), lambda j, i: (0, j)),
            pl.BlockSpec((tm, tn), lambda j, i: (i, g0 + j)),
            pl.BlockSpec((tm, tn), lambda j, i: (i, g0 + nj + j)),
        ],
        out_specs=pl.BlockSpec((tm, tn), lambda j, i: (i, j)),
        out_shape=jax.ShapeDtypeStruct((m, d), BF16),
        compiler_params=_params(2),
        name="merge_branches",
    )(y_a, y_b, w_a, w_b, proj, proj)


def _outproj_kernel(m_ref, w_ref, h_ref, nw_ref, ho_ref, f_ref):
    hn = h_ref[...] + _dot(m_ref[...], w_ref[...])
    ho_ref[...] = hn
    f_ref[...] = _rms(hn, nw_ref[...]).astype(f_ref.dtype)


def out_proj(merged, w_out, h, norm_w):
    m, d = h.shape
    tm = _pick(m, 256, 16)
    return pl.pallas_call(
        _outproj_kernel,
        grid=(m // tm,),
        in_specs=[
            pl.BlockSpec((tm, d), lambda i: (i, 0)),
            pl.BlockSpec((d, d), lambda i: (0, 0)),
            pl.BlockSpec((tm, d), lambda i: (i, 0)),
            pl.BlockSpec((1, d), lambda i: (0, 0)),
        ],
        out_specs=[pl.BlockSpec((tm, d), lambda i: (i, 0)), pl.BlockSpec((tm, d), lambda i: (i, 0))],
        out_shape=[jax.ShapeDtypeStruct((m, d), F32), jax.ShapeDtypeStruct((m, d), BF16)],
        compiler_params=_params(1),
        name="out_proj",
    )(merged, w_out, h, norm_w.reshape(1, d))


def _mlp_kernel(se_ref, ns_ref, x_ref, wg_ref, wu_ref, wd_ref, *rest, n_sub, residual):
    if residual:
        res_ref, o_ref = rest
    else:
        res_ref = None
        (o_ref,) = rest
    s = pl.program_id(0)
    j = pl.program_id(1)
    n = ns_ref[s]
    n_pairs = n // 2
    odd = n - 2 * n_pairs

    def rows_of(block, n_blocks):
        return pl.ds(pl.multiple_of(block * SUB_ROWS, SUB_ROWS), n_blocks * SUB_ROWS)

    def update(rows, first):
        x = x_ref[rows, :]
        g = _dot(x, wg_ref[...].astype(BF16))
        u = _dot(x, wu_ref[...].astype(BF16))
        y = _dot((_silu(g) * u).astype(BF16), wd_ref[...].astype(BF16))
        if not first:
            o_ref[rows, :] += y
        elif residual:
            o_ref[rows, :] = y + res_ref[rows, :]
        else:
            o_ref[rows, :] = y

    def sweep(first):
        def pair(p, carry):
            update(rows_of(2 * p, 2), first)
            return carry

        lax.fori_loop(0, n_pairs, pair, 0)

        @pl.when(odd == 1)
        def _():
            update(rows_of(2 * n_pairs, 1), first)

    @pl.when(j == 0)
    def _():
        sweep(True)

        def blank(r, carry):
            o_ref[rows_of(r, 1), :] = jnp.zeros((SUB_ROWS, o_ref.shape[1]), F32)
            return carry

        lax.fori_loop(n, n_sub, blank, 0)

    @pl.when(j > 0)
    def _():
        sweep(False)


def fused_mlp(x, w_gate, w_up, w_down, tile_expert, tile_sub, tm, tf, residual=None):
    r, d = x.shape
    e, _, f = w_gate.shape
    n_tiles = r // tm
    n_sub = tm // SUB_ROWS
    nf = f // tf
    assert r % tm == 0 and tm % SUB_ROWS == 0 and f % tf == 0

    def wcol(s, j, se, ns):
        return (se[s], 0, jnp.where(ns[s] > 0, j, nf - 1))

    def wrow(s, j, se, ns):
        return (se[s], jnp.where(ns[s] > 0, j, nf - 1), 0)

    in_specs = [
        pl.BlockSpec((tm, d), lambda s, j, se, ns: (s, 0)),
        pl.BlockSpec((None, d, tf), wcol),
        pl.BlockSpec((None, d, tf), wcol),
        pl.BlockSpec((None, tf, d), wrow),
    ]
    args = [x, w_gate, w_up, w_down]
    if residual is not None:
        in_specs.append(pl.BlockSpec((tm, d), lambda s, j, se, ns: (s, 0)))
        args.append(residual)
    kern = functools.partial(_mlp_kernel, n_sub=n_sub, residual=residual is not None)
    return pl.pallas_call(
        kern,
        grid_spec=pltpu.PrefetchScalarGridSpec(
            num_scalar_prefetch=2,
            grid=(n_tiles, nf),
            in_specs=in_specs,
            out_specs=pl.BlockSpec((tm, d), lambda s, j, se, ns: (s, 0)),
        ),
        out_shape=jax.ShapeDtypeStruct((r, d), F32),
        compiler_params=_params(2),
        name="fused_mlp",
    )(tile_expert, tile_sub, *args)


def _router_kernel(h_ref, nw_ref, r_ref, f_ref, idx_ref, gate_ref, *, n_exp):
    f = _rms(h_ref[...], nw_ref[...])
    f_ref[...] = f
    f_hi = f.astype(BF16)
    f_lo = (f - f_hi.astype(F32)).astype(BF16)
    r = r_ref[...]
    r_hi = r.astype(BF16)
    r_lo = (r - r_hi.astype(F32)).astype(BF16)
    logits = _dot(f_hi, r_hi) + (_dot(f_lo, r_hi) + _dot(f_hi, r_lo))
    lane = lax.broadcasted_iota(jnp.int32, logits.shape, 1)
    m1 = jnp.max(logits, axis=-1, keepdims=True)
    i1 = jnp.min(jnp.where(logits == m1, lane, n_exp), axis=-1, keepdims=True)
    rest = jnp.where(lane == i1, -jnp.inf, logits)
    m2 = jnp.max(rest, axis=-1, keepdims=True)
    i2 = jnp.min(jnp.where(rest == m2, lane, n_exp), axis=-1, keepdims=True)
    ex = jnp.exp(m2 - m1)
    den = 1.0 + ex
    two = lax.broadcasted_iota(jnp.int32, idx_ref.shape, 1)
    idx_ref[...] = jnp.where(two == 0, i1, i2)
    gate_ref[...] = jnp.where(two == 0, 1.0 / den, ex / den)


def router_top2(h, norm_w, router_w):
    m, d = h.shape
    n_exp = router_w.shape[1]
    tm = _pick(m, 512, 16)
    kern = functools.partial(_router_kernel, n_exp=n_exp)
    return pl.pallas_call(
        kern,
        grid=(m // tm,),
        in_specs=[
            pl.BlockSpec((tm, d), lambda i: (i, 0)),
            pl.BlockSpec((1, d), lambda i: (0, 0)),
            pl.BlockSpec((d, n_exp), lambda i: (0, 0)),
        ],
        out_specs=[
            pl.BlockSpec((tm, d), lambda i: (i, 0)),
            pl.BlockSpec((tm, TOP_K), lambda i: (i, 0)),
            pl.BlockSpec((tm, TOP_K), lambda i: (i, 0)),
        ],
        out_shape=[
            jax.ShapeDtypeStruct((m, d), F32),
            jax.ShapeDtypeStruct((m, TOP_K), jnp.int32),
            jax.ShapeDtypeStruct((m, TOP_K), F32),
        ],
        compiler_params=_params(1),
        name="router_top2",
    )(h, norm_w.reshape(1, d), router_w)


def _gather_kernel(ns_ref, idx_ref, src_ref, dst_ref, buf, sem, *, n_sub):
    n = ns_ref[pl.program_id(0)]
    n_rows = n * SUB_ROWS

    def row_copy(r):
        return pltpu.make_async_copy(src_ref.at[pl.ds(idx_ref[r], 1)], buf.at[pl.ds(r, 1)], sem)

    def start(r, carry):
        row_copy(r).start()
        return carry

    def wait(r, carry):
        row_copy(r).wait()
        return carry

    def rows_of(b):
        return pl.ds(pl.multiple_of(b * SUB_ROWS, SUB_ROWS), SUB_ROWS)

    def blank(b, carry):
        dst_ref[rows_of(b), :] = jnp.zeros((SUB_ROWS, dst_ref.shape[1]), dst_ref.dtype)
        return carry

    def cast(b, carry):
        dst_ref[rows_of(b), :] = buf[rows_of(b), :].astype(dst_ref.dtype)
        return carry

    lax.fori_loop(0, n_rows, start, 0)
    lax.fori_loop(n, n_sub, blank, 0)
    lax.fori_loop(0, n_rows, wait, 0)
    lax.fori_loop(0, n, cast, 0)


def gather_rows(src, idx, tile_sub, tm, out_dtype=BF16):
    m, d = src.shape
    r = idx.shape[0]
    assert r % tm == 0 and src.dtype == F32
    kern = functools.partial(_gather_kernel, n_sub=tm // SUB_ROWS)
    return pl.pallas_call(
        kern,
        grid_spec=pltpu.PrefetchScalarGridSpec(
            num_scalar_prefetch=1,
            grid=(r // tm,),
            in_specs=[
                pl.BlockSpec((tm,), lambda s, ns: (s,), memory_space=pltpu.SMEM),
                pl.BlockSpec(memory_space=pl.ANY),
            ],
            out_specs=pl.BlockSpec((tm, d), lambda s, ns: (s, 0)),
            scratch_shapes=[pltpu.VMEM((tm, d), F32), pltpu.SemaphoreType.DMA(())],
        ),
        out_shape=jax.ShapeDtypeStruct((r, d), out_dtype),
        compiler_params=_params(1),
        name="gather_rows",
    )(tile_sub, idx, src)


def _combine_kernel(pos_ref, o_ref, g_ref, h_ref, nw_ref, ya_ref, yb_ref, buf, sem, *, tc, n_a):
    def row_copy(t, k):
        return pltpu.make_async_copy(o_ref.at[pl.ds(pos_ref[TOP_K * t + k], 1)], buf.at[k, pl.ds(t, 1)], sem)

    def start(t, carry):
        for k in range(TOP_K):
            row_copy(t, k).start()
        return carry

    def wait(t, carry):
        for k in range(TOP_K):
            row_copy(t, k).wait()
        return carry

    lax.fori_loop(0, tc, start, 0)
    lax.fori_loop(0, tc, wait, 0)
    g = g_ref[...]
    acc = h_ref[...]
    for k in range(TOP_K):
        acc = acc + g[:, k:k + 1] * buf[k]
    y = _rms(acc, nw_ref[...])
    i = pl.program_id(0)

    @pl.when(i < n_a)
    def _():
        ya_ref[...] = y

    @pl.when(i >= n_a)
    def _():
        yb_ref[...] = y


def combine_experts(expert_out, pos, gates, h, norm_w, rows_a):
    m, d = h.shape
    tc = _pick(np.gcd(rows_a, m - rows_a), 256, 8)
    n_a = rows_a // tc
    kern = functools.partial(_combine_kernel, tc=tc, n_a=n_a)
    return pl.pallas_call(
        kern,
        grid=(m // tc,),
        in_specs=[
            pl.BlockSpec((TOP_K * tc,), lambda i: (i,), memory_space=pltpu.SMEM),
            pl.BlockSpec(memory_space=pl.ANY),
            pl.BlockSpec((tc, TOP_K), lambda i: (i, 0)),
            pl.BlockSpec((tc, d), lambda i: (i, 0)),
            pl.BlockSpec((1, d), lambda i: (0, 0)),
        ],
        out_specs=[
            pl.BlockSpec((tc, d), lambda i: (jnp.minimum(i, n_a - 1), 0)),
            pl.BlockSpec((tc, d), lambda i: (jnp.maximum(i - n_a, 0), 0)),
        ],
        out_shape=[jax.ShapeDtypeStruct((rows_a, d), F32), jax.ShapeDtypeStruct((m - rows_a, d), F32)],
        scratch_shapes=[pltpu.VMEM((TOP_K, tc, d), F32), pltpu.SemaphoreType.DMA(())],
        compiler_params=_params(1),
        name="combine_experts",
    )(pos.reshape(-1), expert_out, gates, h, norm_w.reshape(1, d))


def routing_tables(top_i, n_exp, tm):
    m = top_i.shape[0]
    n_pairs = m * TOP_K
    n_tiles = n_pairs // tm + n_exp
    e_flat = top_i.reshape(-1)
    onehot = (e_flat[:, None] == jnp.arange(n_exp, dtype=jnp.int32)[None, :]).astype(jnp.int32)
    incl = jnp.cumsum(onehot, axis=0)
    counts = incl[-1]
    rank = jnp.sum((incl - onehot) * onehot, axis=1)
    tiles_e = (counts + tm - 1) // tm
    tile_end = jnp.cumsum(tiles_e)
    tile_start = tile_end - tiles_e
    dest = (tile_start * tm)[e_flat] + rank
    src_tok = jnp.zeros((n_tiles * tm,), jnp.int32).at[dest].set(jnp.arange(n_pairs, dtype=jnp.int32) // TOP_K)
    tile = jnp.arange(n_tiles, dtype=jnp.int32)
    used = tile < tile_end[-1]
    t_exp = jnp.minimum(jnp.sum((tile[:, None] >= tile_end[None, :]).astype(jnp.int32), axis=1), n_exp - 1)
    last_exp = t_exp[jnp.maximum(tile_end[-1] - 1, 0)]
    t_exp = jnp.where(used, t_exp, last_exp)
    rows_left = counts[t_exp] - (tile - tile_start[t_exp]) * tm
    t_sub = jnp.where(used, (jnp.clip(rows_left, 0, tm) + SUB_ROWS - 1) // SUB_ROWS, 0)
    return src_tok, dest.reshape(m, TOP_K), t_exp.astype(jnp.int32), t_sub.astype(jnp.int32)


def moe_layer(h, norm_w, router_w, w_gate, w_up, w_down, final_norm_w, rows_a):
    m, d = h.shape
    n_exp = router_w.shape[1]
    f = w_gate.shape[-1]
    tm = 1024 if m * TOP_K >= 8192 else SUB_ROWS
    tf = _pick(f, 256, LANES)
    f_norm, top_i, gates = router_top2(h, norm_w, router_w)
    src_tok, pos, t_exp, t_sub = routing_tables(top_i, n_exp, tm)
    x_sorted = gather_rows(f_norm, src_tok, t_sub, tm)
    expert_out = fused_mlp(x_sorted, w_gate, w_up, w_down, t_exp, t_sub, tm, tf)
    return combine_experts(expert_out, pos, gates, h, final_norm_w, rows_a)


def _tail_rows(prev, proj, row0, t, col0):
    b, n, width = prev.shape
    k = min(n, t)
    if k == t:
        new = lax.slice(proj, (row0, col0), (row0 + b * t, col0 + width)).reshape(b, t, width)
    else:
        new = jnp.stack([lax.slice(proj, (row0 + i * t + t - k, col0), (row0 + (i + 1) * t, col0 + width))
                         for i in range(b)])
    return new if k == n else jnp.concatenate([prev[:, t:], new], axis=1)


def dense_ffn(f_bf, h, w_gate, w_up, w_down):
    m, d = h.shape
    f = w_gate.shape[-1]
    tm = _pick(m, 512, SUB_ROWS)
    tf = _pick(f, 512, LANES)
    n_tiles = m // tm
    t_exp = jnp.zeros((n_tiles,), jnp.int32)
    t_sub = jnp.full((n_tiles,), tm // SUB_ROWS, jnp.int32)
    return fused_mlp(f_bf, w_gate[None], w_up[None], w_down[None], t_exp, t_sub, tm, tf, residual=h)


def kernel(x_prompt, x_sample, state_pool, state_conv, state_rec, norm_mix, w_in, pool_w, pool_scale, conv_w,
           a_log, dt_bias, o_norm, w_branch_a, w_branch_b, w_out, norm_ffn, ffn_w_gate, ffn_w_up, ffn_w_down,
           router, moe_w_gate, moe_w_up, moe_w_down, norm_final):
    bp, tp, d = x_prompt.shape
    bs, ts, _ = x_sample.shape
    depth = w_in.shape[0]
    assert depth == 2, "layer 0 uses the dense FFN, layer 1 the expert FFN followed by the final norm"
    dp = state_pool.shape[-1]
    _, _, nh, hk, hv = state_rec.shape
    dq, dv = nh * hk, nh * hv
    taps = conv_w.shape[1]
    mp, ms = bp * tp, bs * ts
    n_main = dp + 2 * dq + 2 * dv
    col_ab = n_main
    col_gate = n_main + 2 * nh
    assert w_in.shape[-1] == col_gate + 2 * d

    h = jnp.concatenate([x_prompt.reshape(mp, d), x_sample.reshape(ms, d)], axis=0)
    zeros_pool = jnp.zeros((bp,) + state_pool.shape[2:], F32)
    zeros_conv = jnp.zeros((bp,) + state_conv.shape[2:], F32)
    zeros_rec = jnp.zeros((bp,) + state_rec.shape[2:], F32)
    pools_p, convs_p, recs_p, pools_s, convs_s, recs_s = [], [], [], [], [], []
    y = None
    for layer in range(depth):
        w_l = w_in[layer]
        w_ab = jnp.pad(w_l[:, col_ab:col_gate], ((0, 0), (0, LANES - 2 * nh)))
        a = rms_cast(h, norm_mix[layer])
        proj = matmul(a, w_l, n_main)
        ab = matmul(a, w_ab, LANES)
        gates = matmul(a, w_l[:, col_gate:], 2 * d)
        pw = pool_w[layer].astype(BF16)
        ya_p = pool_mixer(proj, 0, bp, tp, zeros_pool, pw, pool_scale[layer], 0)
        ya_s = pool_mixer(proj, mp, bs, ts, state_pool[layer], pw, pool_scale[layer], PAST_LEN)
        yb_p, rec_p = delta_mixer(proj, ab, 0, bp, tp, dp, zeros_conv, zeros_rec, conv_w[layer], a_log[layer],
                                  dt_bias[layer], o_norm[layer])
        yb_s, rec_s = delta_mixer(proj, ab, mp, bs, ts, dp, state_conv[layer], state_rec[layer], conv_w[layer],
                                  a_log[layer], dt_bias[layer], o_norm[layer])
        merged = merge_branches(jnp.concatenate([ya_p, ya_s], axis=0), jnp.concatenate([yb_p, yb_s], axis=0),
                                w_branch_a[layer].astype(BF16), w_branch_b[layer].astype(BF16), gates, 0)
        h, f_bf = out_proj(merged, w_out[layer].astype(BF16), h, norm_ffn[layer])
        if layer == 0:
            h = dense_ffn(f_bf, h, ffn_w_gate[0], ffn_w_up[0], ffn_w_down[0])
        else:
            y_p, y_s = moe_layer(h, norm_ffn[layer], router[0], moe_w_gate[0], moe_w_up[0], moe_w_down[0],
                                 norm_final, mp)
        pools_p.append(_tail_rows(zeros_pool, proj, 0, tp, 0))
        pools_s.append(_tail_rows(state_pool[layer], proj, mp, ts, 0))
        convs_p.append(_tail_rows(zeros_conv, proj, 0, tp, dp))
        convs_s.append(_tail_rows(state_conv[layer], proj, mp, ts, dp))
        recs_p.append(rec_p)
        recs_s.append(rec_s)
    return (y_p.reshape(bp, tp, d), y_s.reshape(bs, ts, d),
            jnp.stack(pools_p), jnp.stack(convs_p), jnp.stack(recs_p),
            jnp.stack(pools_s), jnp.stack(convs_s), jnp.stack(recs_s))
```

```python
import functools

import jax
import jax.numpy as jnp
import numpy as np
from jax import lax
from jax.experimental import pallas as pl
from jax.experimental.pallas import tpu as pltpu

F32 = jnp.float32
BF16 = jnp.bfloat16
EPS = 1e-6
POOL_WINDOWS = (2, 4, 8, 16)
POOL_HALO = 16
CONV_HALO = 8
CHUNK = 64
TOP_K = 2
PAST_LEN = 16384
LANES = 128
SUB_ROWS = 256
VMEM_BYTES = 56 * 1024 * 1024


def _params(n_axes, vmem=VMEM_BYTES):
    return pltpu.CompilerParams(dimension_semantics=("arbitrary",) * n_axes, vmem_limit_bytes=vmem)


def _pick(n, pref, mult):
    if n <= pref:
        return n
    for c in range(pref, 0, -1):
        if n % c == 0 and c % mult == 0:
            return c
    raise ValueError(f"no tile for {n} (pref {pref}, mult {mult})")


def _dot(a, b):
    return jnp.dot(a, b, preferred_element_type=F32)


def _dot_nt(a, b):
    return lax.dot_general(a, b, (((1,), (1,)), ((), ())), preferred_element_type=F32)


def _dot_tn(a, b):
    return lax.dot_general(a, b, (((0,), (0,)), ((), ())), preferred_element_type=F32)


def _split3(x):
    hi = x.astype(BF16)
    r = x - hi.astype(F32)
    mid = r.astype(BF16)
    lo = (r - mid.astype(F32)).astype(BF16)
    return hi, mid, lo


def _rms(x, w):
    ms = jnp.mean(x * x, axis=-1, keepdims=True)
    return x * lax.rsqrt(ms + EPS) * w


def _silu(x):
    return x * jax.nn.sigmoid(x)


def _two_source_specs(xa, xb, tm, index_args):
    n_a = xa.shape[0] // tm
    d = xa.shape[1]
    assert xa.shape[0] % tm == 0 and xb.shape[0] % tm == 0
    first = lambda *g: (jnp.minimum(index_args(*g), n_a - 1), 0)
    second = lambda *g: (jnp.maximum(index_args(*g) - n_a, 0), 0)
    return n_a, [pl.BlockSpec((tm, d), first), pl.BlockSpec((tm, d), second)]


def _rms_pair_kernel(xa_ref, xb_ref, w_ref, o_ref, *, n_a):
    x = jnp.where(pl.program_id(0) < n_a, xa_ref[...], xb_ref[...])
    o_ref[...] = _rms(x, w_ref[...]).astype(o_ref.dtype)


def rms_cast_pair(xa, xb, w):
    d = xa.shape[1]
    m = xa.shape[0] + xb.shape[0]
    tm = _pick(int(np.gcd(xa.shape[0], xb.shape[0])), 512, 16)
    n_a, specs = _two_source_specs(xa, xb, tm, lambda i: i)
    return pl.pallas_call(
        functools.partial(_rms_pair_kernel, n_a=n_a),
        grid=(m // tm,),
        in_specs=specs + [pl.BlockSpec((1, d), lambda i: (0, 0))],
        out_specs=pl.BlockSpec((tm, d), lambda i: (i, 0)),
        out_shape=jax.ShapeDtypeStruct((m, d), BF16),
        compiler_params=_params(1),
        name="rms_cast_pair",
    )(xa, xb, w.reshape(1, d))


def _add_rms_kernel(h_ref, y_ref, w_ref, ho_ref, o_ref):
    hn = h_ref[...] + y_ref[...]
    ho_ref[...] = hn
    o_ref[...] = _rms(hn, w_ref[...]).astype(o_ref.dtype)


def add_rms_cast(h, y, w):
    m, d = h.shape
    tm = _pick(m, 512, 16)
    row = pl.BlockSpec((tm, d), lambda i: (i, 0))
    return pl.pallas_call(
        _add_rms_kernel,
        grid=(m // tm,),
        in_specs=[row, row, pl.BlockSpec((1, d), lambda i: (0, 0))],
        out_specs=[row, row],
        out_shape=[jax.ShapeDtypeStruct((m, d), F32), jax.ShapeDtypeStruct((m, d), BF16)],
        compiler_params=_params(1),
        name="add_rms_cast",
    )(h, y, w.reshape(1, d))


def _mm_kernel(x_ref, w_ref, o_ref, wb_ref):
    @pl.when(pl.program_id(1) == 0)
    def _():
        wb_ref[...] = w_ref[...].astype(BF16)

    o_ref[...] = _dot(x_ref[...], wb_ref[...])


def matmul(x, w, n, layer=None, tm_pref=512, tn_pref=1024):
    m, k = x.shape
    tm = _pick(m, tm_pref, 16)
    tn = _pick(n, tn_pref, LANES)
    assert w.shape[-1] >= n and w.dtype == F32
    if layer is None:
        w_spec = pl.BlockSpec((k, tn), lambda j, i: (0, j))
    else:
        w_spec = pl.BlockSpec((None, k, tn), lambda j, i: (layer, 0, j))
    return pl.pallas_call(
        _mm_kernel,
        grid=(n // tn, m // tm),
        in_specs=[pl.BlockSpec((tm, k), lambda j, i: (i, 0)), w_spec],
        out_specs=pl.BlockSpec((tm, tn), lambda j, i: (i, j)),
        out_shape=jax.ShapeDtypeStruct((m, n), F32),
        scratch_shapes=[pltpu.VMEM((k, tn), BF16)],
        compiler_params=_params(2),
        name="in_proj",
    )(x, w)


def _pool_kernel(u_ref, st_ref, pw_ref, sc_ref, y_ref, ext_ref, *, nb, tt, pos0, gc, rc, n_steps):
    n = pl.program_id(1)
    dp = gc * len(POOL_WINDOWS)
    buf = POOL_HALO - 1

    @pl.when(n == 0)
    def _():
        ext_ref[:, 0:1, :] = jnp.zeros((nb, 1, dp), F32)
        ext_ref[:, 1:POOL_HALO, :] = st_ref[...]

    ext_ref[:, POOL_HALO:POOL_HALO + tt, :] = u_ref[...].reshape(nb, tt, dp)
    for c0 in range(0, tt, rc):
        row = lax.broadcasted_iota(jnp.int32, (1, rc, 1), 1) + (pos0 + c0) + n * tt
        for gi, w in enumerate(POOL_WINDOWS):
            lanes = slice(gi * gc, (gi + 1) * gc)
            base = POOL_HALO + c0
            cur = ext_ref[:, base:base + rc, lanes]
            acc = cur
            for i in range(1, w):
                acc = acc + ext_ref[:, base - i:base - i + rc, lanes]
            cnt = jnp.minimum(row + 1, w).astype(F32)
            d = acc / cnt - cur
            y = _dot(d.reshape(nb * rc, gc).astype(BF16), pw_ref[gi]) * sc_ref[:, lanes]
            if nb == 1:
                y_ref[c0:c0 + rc, lanes] = y
            else:
                y_ref[:, lanes] = y
    if n_steps > 1:
        ext_ref[:, 0:POOL_HALO, :] = ext_ref[:, tt:tt + POOL_HALO, :]
    del buf


def pool_mixer(proj, row0, b, t, state, pool_w, pool_scale, pos0):
    dp = state.shape[-1]
    gc = dp // len(POOL_WINDOWS)
    if t >= 512:
        nb, tt = 1, 512
    else:
        nb, tt = _pick(b, max(1, 256 // t), 1), t
    assert t % tt == 0 and b % nb == 0 and row0 % (nb * tt) == 0 and tt % 8 == 0
    assert nb == 1 or tt == t
    rc = min(tt, 256) if nb == 1 else tt
    n_steps = t // tt
    rb0 = row0 // (nb * tt)
    kern = functools.partial(_pool_kernel, nb=nb, tt=tt, pos0=pos0, gc=gc, rc=rc, n_steps=n_steps)
    return pl.pallas_call(
        kern,
        grid=(b // nb, n_steps),
        in_specs=[
            pl.BlockSpec((nb * tt, dp), lambda i, n: (rb0 + i * n_steps + n, 0)),
            pl.BlockSpec((nb, POOL_HALO - 1, dp), lambda i, n: (i, 0, 0)),
            pl.BlockSpec((len(POOL_WINDOWS), gc, gc), lambda i, n: (0, 0, 0)),
            pl.BlockSpec((1, dp), lambda i, n: (0, 0)),
        ],
        out_specs=pl.BlockSpec((nb * tt, dp), lambda i, n: (i * n_steps + n, 0)),
        out_shape=jax.ShapeDtypeStruct((b * t, dp), F32),
        scratch_shapes=[pltpu.VMEM((nb, tt + POOL_HALO, dp), F32)],
        compiler_params=_params(2),
        name="pool_mixer",
    )(proj, state, pool_w, pool_scale.reshape(1, dp))


def _delta_kernel(q_ref, k_ref, v_ref, z_ref, ab_ref, cs_ref, rs_ref, cw_ref, al_ref, dtb_ref, on_ref,
                  y_ref, so_ref, ext_ref, s_ref, *, nseq, tc, nh, hk, hv, n_steps, taps):
    n = pl.program_id(1)
    c = nseq * tc
    dq = nh * hk
    halo0 = CONV_HALO - (taps - 1)

    @pl.when(n == 0)
    def _():
        ext_ref[:, halo0:CONV_HALO, :] = cs_ref[...]
        for h in range(nh):
            for b in range(nseq):
                s_ref[h, b * hk:(b + 1) * hk, :] = rs_ref[b, h]

    ext_ref[:, CONV_HALO:CONV_HALO + tc, 0:dq] = q_ref[...].reshape(nseq, tc, dq)
    ext_ref[:, CONV_HALO:CONV_HALO + tc, dq:2 * dq] = k_ref[...].reshape(nseq, tc, dq)
    ext_ref[:, CONV_HALO:CONV_HALO + tc, 2 * dq:2 * dq + nh * hv] = v_ref[...].reshape(nseq, tc, nh * hv)

    def conv(col0, width):
        acc = ext_ref[:, halo0:halo0 + tc, col0:col0 + width] * cw_ref[0:1, col0:col0 + width]
        for j in range(1, taps):
            acc = acc + ext_ref[:, halo0 + j:halo0 + j + tc, col0:col0 + width] * cw_ref[j:j + 1, col0:col0 + width]
        return _silu(acc).reshape(c, width)

    ri = lax.broadcasted_iota(jnp.int32, (c, c), 0)
    ci = lax.broadcasted_iota(jnp.int32, (c, c), 1)
    same = (ri // tc) == (ci // tc)
    low = same & (ci <= ri)
    strict = same & (ci < ri)
    eye = (ri == ci).astype(F32)

    ab = ab_ref[...]
    g_full = -jnp.exp(al_ref[...]) * jax.nn.softplus(ab + dtb_ref[...])
    beta_full = jax.nn.sigmoid(ab)
    sel = jnp.concatenate([low.astype(BF16), same.astype(BF16)], axis=0)
    parts = _split3(g_full)
    sums = _dot(sel, parts[0]) + _dot(sel, parts[1]) + _dot(sel, parts[2])
    gcum_full, glast_full = sums[:c], sums[c:]
    pick = (lax.broadcasted_iota(jnp.int32, (8, LANES), 0) == lax.broadcasted_iota(jnp.int32, (8, LANES), 1)).astype(BF16)
    cparts = _split3(gcum_full)
    gcum_t = _dot_nt(pick, cparts[0]) + _dot_nt(pick, cparts[1]) + _dot_nt(pick, cparts[2])

    if nseq > 1:
        rseq = lax.broadcasted_iota(jnp.int32, (c, 1), 0) // tc
        rseq2 = lax.broadcasted_iota(jnp.int32, (2 * c, 1), 0) % c // tc

    def expand(x, seq_of_row):
        if nseq == 1:
            return x
        return jnp.concatenate([jnp.where(seq_of_row == b, x, 0.0) for b in range(nseq)], axis=1)

    levels = int(np.log2(tc))
    assert 2 ** levels == tc
    heads = range(nh)
    s_in = [s_ref[h] for h in heads]
    z_in = [z_ref[:, h * hv:(h + 1) * hv] for h in heads]
    q_c = [conv(h * hk, hk) for h in heads]
    k_c = [conv(dq + h * hk, hk) for h in heads]
    v_c = [conv(2 * dq + h * hv, hv) for h in heads]
    gcs = [gcum_full[:, h:h + 1] for h in heads]
    gls = [glast_full[:, h:h + 1] for h in heads]
    betas = [beta_full[:, nh + h:nh + h + 1] for h in heads]
    decays = [jnp.where(low, jnp.exp(jnp.where(low, gcs[h] - gcum_t[h:h + 1, :], 0.0)), 0.0) for h in heads]
    qs = [q_c[h] * lax.rsqrt(jnp.sum(q_c[h] * q_c[h], axis=-1, keepdims=True) + EPS) * (hk ** -0.5) for h in heads]
    ks = [k_c[h] * lax.rsqrt(jnp.sum(k_c[h] * k_c[h], axis=-1, keepdims=True) + EPS) for h in heads]
    kbs = [ks[h] * betas[h] for h in heads]
    kbf = [ks[h].astype(BF16) for h in heads]
    a_kk = [jnp.where(strict, _dot_nt(kbs[h].astype(BF16), kbf[h]) * decays[h], 0.0) for h in heads]
    a_qk = [(_dot_nt(qs[h].astype(BF16), kbf[h]) * decays[h]).astype(BF16) for h in heads]
    ps = [eye - a_kk[h] for h in heads]
    if levels > 1:
        xb = [(-a_kk[h]).astype(BF16) for h in heads]
        yj = [_dot(xb[h], xb[h]) for h in heads]
        for j in range(1, levels):
            yb = [yj[h].astype(BF16) for h in heads]
            if j < levels - 1:
                r = [_dot(yb[h], jnp.concatenate([ps[h].astype(BF16), yb[h]], axis=1)) for h in heads]
                ps = [ps[h] + r[h][:, :c] for h in heads]
                yj = [r[h][:, c:] for h in heads]
            else:
                ps = [ps[h] + _dot(yb[h], ps[h].astype(BF16)) for h in heads]
    egc = [jnp.exp(gcs[h]) for h in heads]
    uw = [_dot(ps[h].astype(BF16), jnp.concatenate([v_c[h] * betas[h], kbs[h] * egc[h]], axis=1).astype(BF16))
          for h in heads]
    lhs = [expand(jnp.concatenate([uw[h][:, hv:], qs[h] * egc[h]], axis=0), rseq2 if nseq > 1 else None).astype(BF16)
           for h in heads]
    ws = [_dot(lhs[h], s_in[h].astype(BF16)) for h in heads]
    v_new = [(uw[h][:, :hv] - ws[h][:c]).astype(BF16) for h in heads]
    o = [ws[h][c:] + _dot(a_qk[h], v_new[h]) for h in heads]
    kd = [expand(ks[h] * jnp.exp(gls[h] - gcs[h]), rseq if nseq > 1 else None).astype(BF16) for h in heads]
    upd = [_dot_tn(kd[h], v_new[h]) for h in heads]
    if nseq == 1:
        sdec = [jnp.exp(gls[h][0:1, :]) for h in heads]
    else:
        sdec = [jnp.concatenate([jnp.broadcast_to(jnp.exp(gls[h][b * tc:b * tc + 1, :]), (hk, 1))
                                 for b in range(nseq)], axis=0) for h in heads]
    s_out = [s_in[h] * sdec[h] + upd[h] for h in heads]
    y_out = [_rms(o[h], on_ref[...]) * _silu(z_in[h]) for h in heads]
    for h in heads:
        s_ref[h] = s_out[h]
        y_ref[:, h * hv:(h + 1) * hv] = y_out[h]

    if n_steps > 1:
        ext_ref[:, halo0:CONV_HALO, :] = ext_ref[:, halo0 + tc:CONV_HALO + tc, :]

    @pl.when(n == n_steps - 1)
    def _():
        for h in range(nh):
            for b in range(nseq):
                so_ref[b, h] = s_ref[h, b * hk:(b + 1) * hk, :]


def delta_mixer(proj, ab, row0, b, t, col_q, conv_state, rec_state, layer, conv_w, a_log, dt_bias, o_norm,
                rec_out, out_layer):
    _, _, nh, hk, hv = rec_state.shape
    dq, dv = nh * hk, nh * hv
    taps = conv_w.shape[0]
    assert dq == dv and col_q % dq == 0 and nh <= 8 and taps - 1 <= CONV_HALO
    if t >= CHUNK:
        nseq, tc = 1, CHUNK
    else:
        tc = t
        nseq = CHUNK // tc
    assert t % tc == 0 and b % nseq == 0 and tc % 8 == 0
    c = nseq * tc
    n_steps = t // tc
    assert row0 % c == 0
    rb0 = row0 // c
    cq = col_q // dq
    pad = lambda v: jnp.zeros((1, LANES), F32).at[0, :nh].set(v.astype(F32))
    body = functools.partial(_delta_kernel, nseq=nseq, tc=tc, nh=nh, hk=hk, hv=hv, n_steps=n_steps, taps=taps)
    row_map = lambda col: (lambda i, n: (rb0 + i * n_steps + n, col))
    n_in = 11
    n_layers = rec_out.shape[0]
    extra_specs, extra_args, aliases = [pl.BlockSpec(memory_space=pl.ANY)], [rec_out], {n_in: 1}
    kern = lambda *refs: body(*refs[:n_in], *refs[n_in + 1:])
    y, s_out = pl.pallas_call(
        kern,
        grid=(b // nseq, n_steps),
        input_output_aliases=aliases,
        in_specs=[
            pl.BlockSpec((c, dq), row_map(cq)),
            pl.BlockSpec((c, dq), row_map(cq + 1)),
            pl.BlockSpec((c, dv), row_map(cq + 2)),
            pl.BlockSpec((c, dv), row_map(cq + 3)),
            pl.BlockSpec((c, LANES), row_map(0)),
            pl.BlockSpec((nseq, taps - 1, 2 * dq + dv), lambda i, n: (i, 0, 0)),
            pl.BlockSpec((None, nseq, nh, hk, hv), lambda i, n: (layer, i, 0, 0, 0)),
            pl.BlockSpec((taps, 2 * dq + dv), lambda i, n: (0, 0)),
            pl.BlockSpec((1, LANES), lambda i, n: (0, 0)),
            pl.BlockSpec((1, LANES), lambda i, n: (0, 0)),
            pl.BlockSpec((1, hv), lambda i, n: (0, 0)),
        ] + extra_specs,
        out_specs=[
            pl.BlockSpec((c, dv), lambda i, n: (i * n_steps + n, 0)),
            pl.BlockSpec((None, nseq, nh, hk, hv), lambda i, n: (out_layer, i, 0, 0, 0)),
        ],
        out_shape=[jax.ShapeDtypeStruct((b * t, dv), F32), jax.ShapeDtypeStruct((n_layers, b, nh, hk, hv), F32)],
        scratch_shapes=[
            pltpu.VMEM((nseq, tc + CONV_HALO, 2 * dq + dv), F32),
            pltpu.VMEM((nh, nseq * hk, hv), F32),
        ],
        compiler_params=_params(2),
        name="delta_mixer",
    )(proj, proj, proj, proj, ab, conv_state, rec_state, conv_w, pad(a_log), pad(dt_bias), o_norm.reshape(1, hv),
      *extra_args)
    return y, s_out


def _merge_kernel(ya0_ref, ya1_ref, yb0_ref, yb1_ref, wa_ref, wb_ref, ga_ref, gb_ref, o_ref, wab_ref, wbb_ref,
                  *, n_first):
    i = pl.program_id(1)

    @pl.when(i == 0)
    def _():
        wab_ref[...] = wa_ref[...].astype(BF16)
        wbb_ref[...] = wb_ref[...].astype(BF16)

    first = i < n_first
    y_a = jnp.where(first, ya0_ref[...], ya1_ref[...])
    y_b = jnp.where(first, yb0_ref[...], yb1_ref[...])
    a = _dot(y_a.astype(BF16), wab_ref[...])
    b = _dot(y_b.astype(BF16), wbb_ref[...])
    o_ref[...] = (jax.nn.sigmoid(ga_ref[...]) * a + jax.nn.sigmoid(gb_ref[...]) * b).astype(o_ref.dtype)


def merge_branches(y_a, y_b, w_a, w_b, layer, gates):
    dp, dv = y_a[0].shape[1], y_b[0].shape[1]
    m = y_a[0].shape[0] + y_a[1].shape[0]
    d = w_a.shape[-1]
    tm = _pick(int(np.gcd(y_a[0].shape[0], y_a[1].shape[0])), 512, 16)
    tn = _pick(d, 1024, LANES)
    nj = d // tn
    n_first, a_specs = _two_source_specs(y_a[0], y_a[1], tm, lambda j, i: i)
    _, b_specs = _two_source_specs(y_b[0], y_b[1], tm, lambda j, i: i)
    return pl.pallas_call(
        functools.partial(_merge_kernel, n_first=n_first),
        grid=(nj, m // tm),
        in_specs=a_specs + b_specs + [
            pl.BlockSpec((None, dp, tn), lambda j, i: (layer, 0, j)),
            pl.BlockSpec((None, dv, tn), lambda j, i: (layer, 0, j)),
            pl.BlockSpec((tm, tn), lambda j, i: (i, j)),
            pl.BlockSpec((tm, tn), lambda j, i: (i, nj + j)),
        ],
        out_specs=pl.BlockSpec((tm, tn), lambda j, i: (i, j)),
        out_shape=jax.ShapeDtypeStruct((m, d), BF16),
        scratch_shapes=[pltpu.VMEM((dp, tn), BF16), pltpu.VMEM((dv, tn), BF16)],
        compiler_params=_params(2),
        name="merge_branches",
    )(*y_a, *y_b, w_a, w_b, gates, gates)


def _outproj_kernel(m_ref, w_ref, nw_ref, *rest, n_a):
    if n_a is None:
        h_ref, ho_ref, f_ref = rest
        h = h_ref[...]
    else:
        ha_ref, hb_ref, ho_ref, f_ref = rest
        h = jnp.where(pl.program_id(0) < n_a, ha_ref[...], hb_ref[...])
    hn = h + _dot(m_ref[...], w_ref[...])
    ho_ref[...] = hn
    f_ref[...] = _rms(hn, nw_ref[...]).astype(f_ref.dtype)


def out_proj(merged, w_out, h, norm_w):
    m, d = merged.shape
    row = lambda tm: pl.BlockSpec((tm, d), lambda i: (i, 0))
    if isinstance(h, tuple):
        tm = _pick(int(np.gcd(h[0].shape[0], h[1].shape[0])), 256, 16)
        n_a, h_specs = _two_source_specs(h[0], h[1], tm, lambda i: i)
        h_args = list(h)
    else:
        tm = _pick(m, 256, 16)
        n_a, h_specs, h_args = None, [row(tm)], [h]
    return pl.pallas_call(
        functools.partial(_outproj_kernel, n_a=n_a),
        grid=(m // tm,),
        in_specs=[row(tm), pl.BlockSpec((d, d), lambda i: (0, 0)), pl.BlockSpec((1, d), lambda i: (0, 0))] + h_specs,
        out_specs=[row(tm), row(tm)],
        out_shape=[jax.ShapeDtypeStruct((m, d), F32), jax.ShapeDtypeStruct((m, d), BF16)],
        compiler_params=_params(1),
        name="out_proj",
    )(merged, w_out, norm_w.reshape(1, d), *h_args)


def _mlp_kernel(se_ref, ns_ref, x_ref, wg_ref, wu_ref, wd_ref, o_ref, *, n_sub):
    s = pl.program_id(0)
    j = pl.program_id(1)
    n = ns_ref[s]
    n_pairs = n // 2
    odd = n - 2 * n_pairs

    def rows_of(block, n_blocks):
        return pl.ds(pl.multiple_of(block * SUB_ROWS, SUB_ROWS), n_blocks * SUB_ROWS)

    def update(rows, first):
        x = x_ref[rows, :]
        g = _dot(x, wg_ref[...].astype(BF16))
        u = _dot(x, wu_ref[...].astype(BF16))
        y = _dot((_silu(g) * u).astype(BF16), wd_ref[...].astype(BF16))
        if first:
            o_ref[rows, :] = y
        else:
            o_ref[rows, :] += y

    def sweep(first):
        def pair(p, carry):
            update(rows_of(2 * p, 2), first)
            return carry

        lax.fori_loop(0, n_pairs, pair, 0)

        @pl.when(odd == 1)
        def _():
            update(rows_of(2 * n_pairs, 1), first)

    @pl.when(j == 0)
    def _():
        sweep(True)

        def blank(r, carry):
            o_ref[rows_of(r, 1), :] = jnp.zeros((SUB_ROWS, o_ref.shape[1]), F32)
            return carry

        lax.fori_loop(n, n_sub, blank, 0)

    @pl.when(j > 0)
    def _():
        sweep(False)


def fused_mlp(x, w_gate, w_up, w_down, tile_expert, tile_sub, tm, tf):
    r, d = x.shape
    e, _, f = w_gate.shape
    n_tiles = r // tm
    n_sub = tm // SUB_ROWS
    nf = f // tf
    assert r % tm == 0 and tm % SUB_ROWS == 0 and f % tf == 0

    def wcol(s, j, se, ns):
        return (se[s], 0, jnp.where(ns[s] > 0, j, nf - 1))

    def wrow(s, j, se, ns):
        return (se[s], jnp.where(ns[s] > 0, j, nf - 1), 0)

    in_specs = [
        pl.BlockSpec((tm, d), lambda s, j, se, ns: (s, 0)),
        pl.BlockSpec((None, d, tf), wcol),
        pl.BlockSpec((None, d, tf), wcol),
        pl.BlockSpec((None, tf, d), wrow),
    ]
    return pl.pallas_call(
        functools.partial(_mlp_kernel, n_sub=n_sub),
        grid_spec=pltpu.PrefetchScalarGridSpec(
            num_scalar_prefetch=2,
            grid=(n_tiles, nf),
            in_specs=in_specs,
            out_specs=pl.BlockSpec((tm, d), lambda s, j, se, ns: (s, 0)),
        ),
        out_shape=jax.ShapeDtypeStruct((r, d), F32),
        compiler_params=_params(2),
        name="fused_mlp",
    )(tile_expert, tile_sub, x, w_gate, w_up, w_down)


def _router_kernel(h_ref, nw_ref, r_ref, f_ref, idx_ref, gate_ref, rank_ref, cnt_ref, seen_ref, *, n_exp):
    @pl.when(pl.program_id(0) == 0)
    def _():
        seen_ref[...] = jnp.zeros(seen_ref.shape, F32)

    f = _rms(h_ref[...], nw_ref[...])
    f_ref[...] = f
    f_hi = f.astype(BF16)
    f_lo = (f - f_hi.astype(F32)).astype(BF16)
    r = r_ref[...]
    r_hi = r.astype(BF16)
    r_lo = (r - r_hi.astype(F32)).astype(BF16)
    logits = _dot(f_hi, r_hi) + (_dot(f_lo, r_hi) + _dot(f_hi, r_lo))
    lane = lax.broadcasted_iota(jnp.int32, logits.shape, 1)
    m1 = jnp.max(logits, axis=-1, keepdims=True)
    i1 = jnp.min(jnp.where(logits == m1, lane, n_exp), axis=-1, keepdims=True)
    rest = jnp.where(lane == i1, -jnp.inf, logits)
    m2 = jnp.max(rest, axis=-1, keepdims=True)
    i2 = jnp.min(jnp.where(rest == m2, lane, n_exp), axis=-1, keepdims=True)
    ex = jnp.exp(m2 - m1)
    den = 1.0 + ex
    two = lax.broadcasted_iota(jnp.int32, idx_ref.shape, 1)
    idx_ref[...] = jnp.where(two == 0, i1, i2)
    gate_ref[...] = jnp.where(two == 0, 1.0 / den, ex / den)
    tm = logits.shape[0]
    chosen = (lane == i1) | (lane == i2)
    earlier = (lax.broadcasted_iota(jnp.int32, (tm, tm), 1) < lax.broadcasted_iota(jnp.int32, (tm, tm), 0))
    before = _dot(earlier.astype(BF16), chosen.astype(BF16)) + seen_ref[...]
    rank1 = jnp.sum(jnp.where(lane == i1, before, 0.0), axis=-1, keepdims=True)
    rank2 = jnp.sum(jnp.where(lane == i2, before, 0.0), axis=-1, keepdims=True)
    rank_ref[...] = jnp.where(two == 0, rank1, rank2).astype(jnp.int32)
    seen = seen_ref[...] + jnp.sum(chosen.astype(F32), axis=0, keepdims=True)
    seen_ref[...] = seen
    cnt_ref[...] = seen.astype(jnp.int32)


def router_top2(h, norm_w, router_w):
    m, d = h.shape
    n_exp = router_w.shape[1]
    tm = _pick(m, 512, 16)
    kern = functools.partial(_router_kernel, n_exp=n_exp)
    return pl.pallas_call(
        kern,
        grid=(m // tm,),
        in_specs=[
            pl.BlockSpec((tm, d), lambda i: (i, 0)),
            pl.BlockSpec((1, d), lambda i: (0, 0)),
            pl.BlockSpec((d, n_exp), lambda i: (0, 0)),
        ],
        out_specs=[
            pl.BlockSpec((tm, d), lambda i: (i, 0)),
            pl.BlockSpec((tm, TOP_K), lambda i: (i, 0)),
            pl.BlockSpec((tm, TOP_K), lambda i: (i, 0)),
            pl.BlockSpec((tm, TOP_K), lambda i: (i, 0)),
            pl.BlockSpec((1, n_exp), lambda i: (0, 0)),
        ],
        out_shape=[
            jax.ShapeDtypeStruct((m, d), F32),
            jax.ShapeDtypeStruct((m, TOP_K), jnp.int32),
            jax.ShapeDtypeStruct((m, TOP_K), F32),
            jax.ShapeDtypeStruct((m, TOP_K), jnp.int32),
            jax.ShapeDtypeStruct((1, n_exp), jnp.int32),
        ],
        scratch_shapes=[pltpu.VMEM((1, n_exp), F32)],
        compiler_params=_params(1),
        name="router_top2",
    )(h, norm_w.reshape(1, d), router_w)


def _gather_kernel(ns_ref, idx_ref, src_ref, dst_ref, buf, sem, *, n_sub):
    n = ns_ref[pl.program_id(0)]
    n_rows = n * SUB_ROWS

    def row_copy(r):
        return pltpu.make_async_copy(src_ref.at[pl.ds(idx_ref[r], 1)], buf.at[pl.ds(r, 1)], sem)

    def start(r, carry):
        row_copy(r).start()
        return carry

    def wait(r, carry):
        row_copy(r).wait()
        return carry

    def rows_of(b):
        return pl.ds(pl.multiple_of(b * SUB_ROWS, SUB_ROWS), SUB_ROWS)

    def blank(b, carry):
        dst_ref[rows_of(b), :] = jnp.zeros((SUB_ROWS, dst_ref.shape[1]), dst_ref.dtype)
        return carry

    def cast(b, carry):
        dst_ref[rows_of(b), :] = buf[rows_of(b), :].astype(dst_ref.dtype)
        return carry

    lax.fori_loop(0, n_rows, start, 0)
    lax.fori_loop(n, n_sub, blank, 0)
    lax.fori_loop(0, n_rows, wait, 0)
    lax.fori_loop(0, n, cast, 0)


def gather_rows(src, idx, tile_sub, tm, out_dtype=BF16):
    m, d = src.shape
    r = idx.shape[0]
    assert r % tm == 0 and src.dtype == F32
    kern = functools.partial(_gather_kernel, n_sub=tm // SUB_ROWS)
    return pl.pallas_call(
        kern,
        grid_spec=pltpu.PrefetchScalarGridSpec(
            num_scalar_prefetch=1,
            grid=(r // tm,),
            in_specs=[
                pl.BlockSpec((tm,), lambda s, ns: (s,), memory_space=pltpu.SMEM),
                pl.BlockSpec(memory_space=pl.ANY),
            ],
            out_specs=pl.BlockSpec((tm, d), lambda s, ns: (s, 0)),
            scratch_shapes=[pltpu.VMEM((tm, d), F32), pltpu.SemaphoreType.DMA(())],
        ),
        out_shape=jax.ShapeDtypeStruct((r, d), out_dtype),
        compiler_params=_params(1),
        name="gather_rows",
    )(tile_sub, idx, src)


def _combine_kernel(pos_ref, o_ref, g_ref, h_ref, nw_ref, ya_ref, yb_ref, buf, sem, *, tc, n_a):
    def row_copy(t, k):
        return pltpu.make_async_copy(o_ref.at[pl.ds(pos_ref[TOP_K * t + k], 1)], buf.at[k, pl.ds(t, 1)], sem)

    def start(t, carry):
        for k in range(TOP_K):
            row_copy(t, k).start()
        return carry

    def wait(t, carry):
        for k in range(TOP_K):
            row_copy(t, k).wait()
        return carry

    lax.fori_loop(0, tc, start, 0)
    lax.fori_loop(0, tc, wait, 0)
    g = g_ref[...]
    acc = h_ref[...]
    for k in range(TOP_K):
        acc = acc + g[:, k:k + 1] * buf[k]
    y = _rms(acc, nw_ref[...])
    i = pl.program_id(0)

    @pl.when(i < n_a)
    def _():
        ya_ref[...] = y

    @pl.when(i >= n_a)
    def _():
        yb_ref[...] = y


def combine_experts(expert_out, pos, gates, h, norm_w, rows_a):
    m, d = h.shape
    tc = _pick(np.gcd(rows_a, m - rows_a), 256, 8)
    n_a = rows_a // tc
    kern = functools.partial(_combine_kernel, tc=tc, n_a=n_a)
    return pl.pallas_call(
        kern,
        grid=(m // tc,),
        in_specs=[
            pl.BlockSpec((TOP_K * tc,), lambda i: (i,), memory_space=pltpu.SMEM),
            pl.BlockSpec(memory_space=pl.ANY),
            pl.BlockSpec((tc, TOP_K), lambda i: (i, 0)),
            pl.BlockSpec((tc, d), lambda i: (i, 0)),
            pl.BlockSpec((1, d), lambda i: (0, 0)),
        ],
        out_specs=[
            pl.BlockSpec((tc, d), lambda i: (jnp.minimum(i, n_a - 1), 0)),
            pl.BlockSpec((tc, d), lambda i: (jnp.maximum(i - n_a, 0), 0)),
        ],
        out_shape=[jax.ShapeDtypeStruct((rows_a, d), F32), jax.ShapeDtypeStruct((m - rows_a, d), F32)],
        scratch_shapes=[pltpu.VMEM((TOP_K, tc, d), F32), pltpu.SemaphoreType.DMA(())],
        compiler_params=_params(1),
        name="combine_experts",
    )(pos.reshape(-1), expert_out, gates, h, norm_w.reshape(1, d))


def routing_tables(top_i, rank, counts, tm):
    m = top_i.shape[0]
    n_exp = counts.shape[0]
    n_pairs = m * TOP_K
    n_tiles = n_pairs // tm + n_exp
    e_flat = top_i.reshape(-1)
    rank = rank.reshape(-1)
    tiles_e = (counts + tm - 1) // tm
    tile_end = jnp.cumsum(tiles_e)
    tile_start = tile_end - tiles_e
    dest = (tile_start * tm)[e_flat] + rank
    src_tok = jnp.zeros((n_tiles * tm,), jnp.int32).at[dest].set(jnp.arange(n_pairs, dtype=jnp.int32) // TOP_K)
    tile = jnp.arange(n_tiles, dtype=jnp.int32)
    used = tile < tile_end[-1]
    t_exp = jnp.minimum(jnp.sum((tile[:, None] >= tile_end[None, :]).astype(jnp.int32), axis=1), n_exp - 1)
    last_exp = t_exp[jnp.maximum(tile_end[-1] - 1, 0)]
    t_exp = jnp.where(used, t_exp, last_exp)
    rows_left = counts[t_exp] - (tile - tile_start[t_exp]) * tm
    t_sub = jnp.where(used, (jnp.clip(rows_left, 0, tm) + SUB_ROWS - 1) // SUB_ROWS, 0)
    return src_tok, dest.reshape(m, TOP_K), t_exp.astype(jnp.int32), t_sub.astype(jnp.int32)


def moe_layer(h, norm_w, router_w, w_gate, w_up, w_down, final_norm_w, rows_a):
    m, d = h.shape
    n_exp = router_w.shape[1]
    f = w_gate.shape[-1]
    tm = 1024 if m * TOP_K >= 8192 else SUB_ROWS
    tf = _pick(f, 256, LANES)
    f_norm, top_i, gates, rank, counts = router_top2(h, norm_w, router_w)
    src_tok, pos, t_exp, t_sub = routing_tables(top_i, rank, counts[0], tm)
    x_sorted = gather_rows(f_norm, src_tok, t_sub, tm)
    expert_out = fused_mlp(x_sorted, w_gate, w_up, w_down, t_exp, t_sub, tm, tf)
    return combine_experts(expert_out, pos, gates, h, final_norm_w, rows_a)


def _tail_rows(prev, proj, row0, t, col0):
    b, n, width = prev.shape
    k = min(n, t)
    if k == t or b * t <= 8 * b * k:
        new = lax.slice(proj, (row0, col0), (row0 + b * t, col0 + width)).reshape(b, t, width)[:, t - k:]
    else:
        new = jnp.stack([lax.slice(proj, (row0 + i * t + t - k, col0), (row0 + (i + 1) * t, col0 + width))
                         for i in range(b)])
    return new if k == n else jnp.concatenate([prev[:, t:], new], axis=1)


def dense_ffn(f_bf, w_gate, w_up, w_down):
    m, d = f_bf.shape
    f = w_gate.shape[-1]
    tm = _pick(m, 1024, 2 * SUB_ROWS)
    tf = _pick(f, 256, LANES)
    n_tiles = m // tm
    t_exp = jnp.zeros((n_tiles,), jnp.int32)
    t_sub = jnp.full((n_tiles,), tm // SUB_ROWS, jnp.int32)
    return fused_mlp(f_bf, w_gate[None], w_up[None], w_down[None], t_exp, t_sub, tm, tf)


def kernel(x_prompt, x_sample, state_pool, state_conv, state_rec, norm_mix, w_in, pool_w, pool_scale, conv_w,
           a_log, dt_bias, o_norm, w_branch_a, w_branch_b, w_out, norm_ffn, ffn_w_gate, ffn_w_up, ffn_w_down,
           router, moe_w_gate, moe_w_up, moe_w_down, norm_final):
    bp, tp, d = x_prompt.shape
    bs, ts, _ = x_sample.shape
    depth = w_in.shape[0]
    assert depth == 2, "layer 0 uses the dense FFN, layer 1 the expert FFN followed by the final norm"
    dp = state_pool.shape[-1]
    _, _, nh, hk, hv = state_rec.shape
    dq, dv = nh * hk, nh * hv
    taps = conv_w.shape[1]
    mp, ms = bp * tp, bs * ts
    n_main = dp + 2 * dq + 2 * dv
    col_ab = n_main
    col_gate = n_main + 2 * nh
    assert w_in.shape[-1] == col_gate + 2 * d

    x_pair = (x_prompt.reshape(mp, d), x_sample.reshape(ms, d))
    zeros_pool = jnp.zeros((bp,) + state_pool.shape[2:], F32)
    zeros_conv = jnp.zeros((bp,) + state_conv.shape[2:], F32)
    zeros_rec = jnp.zeros((1, bp) + state_rec.shape[2:], F32)
    pools_p, convs_p, pools_s, convs_s = [], [], [], []
    recs_p = jnp.zeros((depth, bp) + state_rec.shape[2:], F32)
    recs_s = jnp.zeros(state_rec.shape, F32)
    h = ffn_out = None
    for layer in range(depth):
        w_ab = jnp.pad(w_in[layer, :, col_ab:col_gate], ((0, 0), (0, LANES - 2 * nh)))
        if layer == 0:
            a = rms_cast_pair(*x_pair, norm_mix[layer])
        else:
            h, a = add_rms_cast(h, ffn_out, norm_mix[layer])
        proj = matmul(a, w_in, n_main, layer=layer)
        ab = matmul(a, w_ab, LANES)
        gates = matmul(a, w_in[layer, :, col_gate:], 2 * d)
        pw = pool_w[layer].astype(BF16)
        ya_p = pool_mixer(proj, 0, bp, tp, zeros_pool, pw, pool_scale[layer], 0)
        ya_s = pool_mixer(proj, mp, bs, ts, state_pool[layer], pw, pool_scale[layer], PAST_LEN)
        yb_p, recs_p = delta_mixer(proj, ab, 0, bp, tp, dp, zeros_conv, zeros_rec, 0, conv_w[layer], a_log[layer],
                                   dt_bias[layer], o_norm[layer], recs_p, layer)
        yb_s, recs_s = delta_mixer(proj, ab, mp, bs, ts, dp, state_conv[layer], state_rec, layer, conv_w[layer],
                                   a_log[layer], dt_bias[layer], o_norm[layer], recs_s, layer)
        merged = merge_branches((ya_p, ya_s), (yb_p, yb_s), w_branch_a, w_branch_b, layer, gates)
        h, f_bf = out_proj(merged, w_out[layer].astype(BF16), x_pair if layer == 0 else h, norm_ffn[layer])
        if layer == 0:
            ffn_out = dense_ffn(f_bf, ffn_w_gate[0], ffn_w_up[0], ffn_w_down[0])
        else:
            y_p, y_s = moe_layer(h, norm_ffn[layer], router[0], moe_w_gate[0], moe_w_up[0], moe_w_down[0],
                                 norm_final, mp)
        pools_p.append(_tail_rows(zeros_pool, proj, 0, tp, 0))
        pools_s.append(_tail_rows(state_pool[layer], proj, mp, ts, 0))
        convs_p.append(_tail_rows(zeros_conv, proj, 0, tp, dp))
        convs_s.append(_tail_rows(state_conv[layer], proj, mp, ts, dp))
    return (y_p.reshape(bp, tp, d), y_s.reshape(bs, ts, d),
            jnp.stack(pools_p), jnp.stack(convs_p), recs_p,
            jnp.stack(pools_s), jnp.stack(convs_s), recs_s)
```

```python
import functools

import jax
import jax.numpy as jnp
import numpy as np
from jax import lax
from jax.experimental import pallas as pl
from jax.experimental.pallas import tpu as pltpu

F32 = jnp.float32
BF16 = jnp.bfloat16
EPS = 1e-6
POOL_WINDOWS = (2, 4, 8, 16)
POOL_HALO = 16
CONV_HALO = 8
CHUNK = 64
TOP_K = 2
PAST_LEN = 16384
LANES = 128
SUB_ROWS = 256
DMA_UNROLL = 8
VMEM_BYTES = 56 * 1024 * 1024


def _params(n_axes, vmem=VMEM_BYTES):
    return pltpu.CompilerParams(dimension_semantics=("arbitrary",) * n_axes, vmem_limit_bytes=vmem)


def _pick(n, pref, mult):
    if n <= pref:
        return n
    for c in range(pref, 0, -1):
        if n % c == 0 and c % mult == 0:
            return c
    raise ValueError(f"no tile for {n} (pref {pref}, mult {mult})")


def _dot(a, b):
    return jnp.dot(a, b, preferred_element_type=F32)


def _dot_nt(a, b):
    return lax.dot_general(a, b, (((1,), (1,)), ((), ())), preferred_element_type=F32)


def _dot_tn(a, b):
    return lax.dot_general(a, b, (((0,), (0,)), ((), ())), preferred_element_type=F32)


def _split3(x):
    hi = x.astype(BF16)
    r = x - hi.astype(F32)
    mid = r.astype(BF16)
    lo = (r - mid.astype(F32)).astype(BF16)
    return hi, mid, lo


def _rms(x, w):
    ms = jnp.mean(x * x, axis=-1, keepdims=True)
    return x * lax.rsqrt(ms + EPS) * w


def _silu(x):
    return x * jax.nn.sigmoid(x)


def _two_source_specs(xa, xb, tm, index_args):
    n_a = xa.shape[0] // tm
    d = xa.shape[1]
    assert xa.shape[0] % tm == 0 and xb.shape[0] % tm == 0
    first = lambda *g: (jnp.minimum(index_args(*g), n_a - 1), 0)
    second = lambda *g: (jnp.maximum(index_args(*g) - n_a, 0), 0)
    return n_a, [pl.BlockSpec((tm, d), first), pl.BlockSpec((tm, d), second)]


def _rms_pair_kernel(xa_ref, xb_ref, w_ref, o_ref, *, n_a):
    x = jnp.where(pl.program_id(0) < n_a, xa_ref[...], xb_ref[...])
    o_ref[...] = _rms(x, w_ref[...]).astype(o_ref.dtype)


def rms_cast_pair(xa, xb, w):
    d = xa.shape[1]
    m = xa.shape[0] + xb.shape[0]
    tm = _pick(int(np.gcd(xa.shape[0], xb.shape[0])), 512, 16)
    n_a, specs = _two_source_specs(xa, xb, tm, lambda i: i)
    return pl.pallas_call(
        functools.partial(_rms_pair_kernel, n_a=n_a),
        grid=(m // tm,),
        in_specs=specs + [pl.BlockSpec((1, d), lambda i: (0, 0))],
        out_specs=pl.BlockSpec((tm, d), lambda i: (i, 0)),
        out_shape=jax.ShapeDtypeStruct((m, d), BF16),
        compiler_params=_params(1),
        name="rms_cast_pair",
    )(xa, xb, w.reshape(1, d))


def _add_rms_kernel(h_ref, y_ref, w_ref, ho_ref, o_ref):
    hn = h_ref[...] + y_ref[...]
    ho_ref[...] = hn
    o_ref[...] = _rms(hn, w_ref[...]).astype(o_ref.dtype)


def add_rms_cast(h, y, w):
    m, d = h.shape
    tm = _pick(m, 512, 16)
    row = pl.BlockSpec((tm, d), lambda i: (i, 0))
    return pl.pallas_call(
        _add_rms_kernel,
        grid=(m // tm,),
        in_specs=[row, row, pl.BlockSpec((1, d), lambda i: (0, 0))],
        out_specs=[row, row],
        out_shape=[jax.ShapeDtypeStruct((m, d), F32), jax.ShapeDtypeStruct((m, d), BF16)],
        compiler_params=_params(1),
        name="add_rms_cast",
    )(h, y, w.reshape(1, d))


def _mm_kernel(x_ref, w_ref, o_ref, wb_ref):
    @pl.when(pl.program_id(1) == 0)
    def _():
        wb_ref[...] = w_ref[...].astype(BF16)

    o_ref[...] = _dot(x_ref[...], wb_ref[...])


def matmul(x, w, n, layer=None, tm_pref=512, tn_pref=1024):
    m, k = x.shape
    tm = _pick(m, tm_pref, 16)
    tn = _pick(n, tn_pref, LANES)
    assert w.shape[-1] >= n and w.dtype == F32
    if layer is None:
        w_spec = pl.BlockSpec((k, tn), lambda j, i: (0, j))
    else:
        w_spec = pl.BlockSpec((None, k, tn), lambda j, i: (layer, 0, j))
    return pl.pallas_call(
        _mm_kernel,
        grid=(n // tn, m // tm),
        in_specs=[pl.BlockSpec((tm, k), lambda j, i: (i, 0)), w_spec],
        out_specs=pl.BlockSpec((tm, tn), lambda j, i: (i, j)),
        out_shape=jax.ShapeDtypeStruct((m, n), F32),
        scratch_shapes=[pltpu.VMEM((k, tn), BF16)],
        compiler_params=_params(2),
        name="in_proj",
    )(x, w)


def _pool_kernel(u_ref, st_ref, pw_ref, sc_ref, y_ref, ext_ref, *, nb, tt, pos0, gc, rc, n_steps):
    n = pl.program_id(1)
    dp = gc * len(POOL_WINDOWS)
    buf = POOL_HALO - 1

    @pl.when(n == 0)
    def _():
        ext_ref[:, 0:1, :] = jnp.zeros((nb, 1, dp), F32)
        ext_ref[:, 1:POOL_HALO, :] = st_ref[...]

    ext_ref[:, POOL_HALO:POOL_HALO + tt, :] = u_ref[...].reshape(nb, tt, dp)
    for c0 in range(0, tt, rc):
        row = lax.broadcasted_iota(jnp.int32, (1, rc, 1), 1) + (pos0 + c0) + n * tt
        for gi, w in enumerate(POOL_WINDOWS):
            lanes = slice(gi * gc, (gi + 1) * gc)
            base = POOL_HALO + c0
            cur = ext_ref[:, base:base + rc, lanes]
            acc = cur
            for i in range(1, w):
                acc = acc + ext_ref[:, base - i:base - i + rc, lanes]
            cnt = jnp.minimum(row + 1, w).astype(F32)
            d = acc / cnt - cur
            y = _dot(d.reshape(nb * rc, gc).astype(BF16), pw_ref[gi]) * sc_ref[:, lanes]
            if nb == 1:
                y_ref[c0:c0 + rc, lanes] = y
            else:
                y_ref[:, lanes] = y
    if n_steps > 1:
        ext_ref[:, 0:POOL_HALO, :] = ext_ref[:, tt:tt + POOL_HALO, :]
    del buf


def pool_mixer(proj, row0, b, t, state, pool_w, pool_scale, pos0):
    dp = state.shape[-1]
    gc = dp // len(POOL_WINDOWS)
    if t >= 512:
        nb, tt = 1, 512
    else:
        nb, tt = _pick(b, max(1, 256 // t), 1), t
    assert t % tt == 0 and b % nb == 0 and row0 % (nb * tt) == 0 and tt % 8 == 0
    assert nb == 1 or tt == t
    rc = min(tt, 256) if nb == 1 else tt
    n_steps = t // tt
    rb0 = row0 // (nb * tt)
    kern = functools.partial(_pool_kernel, nb=nb, tt=tt, pos0=pos0, gc=gc, rc=rc, n_steps=n_steps)
    return pl.pallas_call(
        kern,
        grid=(b // nb, n_steps),
        in_specs=[
            pl.BlockSpec((nb * tt, dp), lambda i, n: (rb0 + i * n_steps + n, 0)),
            pl.BlockSpec((nb, POOL_HALO - 1, dp), lambda i, n: (i, 0, 0)),
            pl.BlockSpec((len(POOL_WINDOWS), gc, gc), lambda i, n: (0, 0, 0)),
            pl.BlockSpec((1, dp), lambda i, n: (0, 0)),
        ],
        out_specs=pl.BlockSpec((nb * tt, dp), lambda i, n: (i * n_steps + n, 0)),
        out_shape=jax.ShapeDtypeStruct((b * t, dp), F32),
        scratch_shapes=[pltpu.VMEM((nb, tt + POOL_HALO, dp), F32)],
        compiler_params=_params(2),
        name="pool_mixer",
    )(proj, state, pool_w, pool_scale.reshape(1, dp))


def _delta_kernel(q_ref, k_ref, v_ref, z_ref, ab_ref, cs_ref, rs_ref, cw_ref, al_ref, dtb_ref, on_ref,
                  y_ref, so_ref, ext_ref, s_ref, *, nseq, tc, nh, hk, hv, n_steps, taps):
    n = pl.program_id(1)
    c = nseq * tc
    dq = nh * hk
    halo0 = CONV_HALO - (taps - 1)

    @pl.when(n == 0)
    def _():
        ext_ref[:, halo0:CONV_HALO, :] = cs_ref[...]
        for h in range(nh):
            for b in range(nseq):
                s_ref[h, b * hk:(b + 1) * hk, :] = rs_ref[b, h]

    ext_ref[:, CONV_HALO:CONV_HALO + tc, 0:dq] = q_ref[...].reshape(nseq, tc, dq)
    ext_ref[:, CONV_HALO:CONV_HALO + tc, dq:2 * dq] = k_ref[...].reshape(nseq, tc, dq)
    ext_ref[:, CONV_HALO:CONV_HALO + tc, 2 * dq:2 * dq + nh * hv] = v_ref[...].reshape(nseq, tc, nh * hv)

    def conv(col0, width):
        acc = ext_ref[:, halo0:halo0 + tc, col0:col0 + width] * cw_ref[0:1, col0:col0 + width]
        for j in range(1, taps):
            acc = acc + ext_ref[:, halo0 + j:halo0 + j + tc, col0:col0 + width] * cw_ref[j:j + 1, col0:col0 + width]
        return _silu(acc).reshape(c, width)

    ri = lax.broadcasted_iota(jnp.int32, (c, c), 0)
    ci = lax.broadcasted_iota(jnp.int32, (c, c), 1)
    same = (ri // tc) == (ci // tc)
    low = same & (ci <= ri)
    strict = same & (ci < ri)
    eye = (ri == ci).astype(F32)

    ab = ab_ref[...]
    g_full = -jnp.exp(al_ref[...]) * jax.nn.softplus(ab + dtb_ref[...])
    beta_full = jax.nn.sigmoid(ab)
    sel = jnp.concatenate([low.astype(BF16), same.astype(BF16)], axis=0)
    parts = _split3(g_full)
    sums = _dot(sel, parts[0]) + _dot(sel, parts[1]) + _dot(sel, parts[2])
    gcum_full, glast_full = sums[:c], sums[c:]
    pick = (lax.broadcasted_iota(jnp.int32, (8, LANES), 0) == lax.broadcasted_iota(jnp.int32, (8, LANES), 1)).astype(BF16)
    cparts = _split3(gcum_full)
    gcum_t = _dot_nt(pick, cparts[0]) + _dot_nt(pick, cparts[1]) + _dot_nt(pick, cparts[2])

    if nseq > 1:
        rseq = lax.broadcasted_iota(jnp.int32, (c, 1), 0) // tc
        rseq2 = lax.broadcasted_iota(jnp.int32, (2 * c, 1), 0) % c // tc

    def expand(x, seq_of_row):
        if nseq == 1:
            return x
        return jnp.concatenate([jnp.where(seq_of_row == b, x, 0.0) for b in range(nseq)], axis=1)

    levels = int(np.log2(tc))
    assert 2 ** levels == tc
    heads = range(nh)
    s_in = [s_ref[h] for h in heads]
    z_in = [z_ref[:, h * hv:(h + 1) * hv] for h in heads]
    q_c = [conv(h * hk, hk) for h in heads]
    k_c = [conv(dq + h * hk, hk) for h in heads]
    v_c = [conv(2 * dq + h * hv, hv) for h in heads]
    gcs = [gcum_full[:, h:h + 1] for h in heads]
    gls = [glast_full[:, h:h + 1] for h in heads]
    betas = [beta_full[:, nh + h:nh + h + 1] for h in heads]
    decays = [jnp.where(low, jnp.exp(jnp.where(low, gcs[h] - gcum_t[h:h + 1, :], 0.0)), 0.0) for h in heads]
    qs = [q_c[h] * lax.rsqrt(jnp.sum(q_c[h] * q_c[h], axis=-1, keepdims=True) + EPS) * (hk ** -0.5) for h in heads]
    ks = [k_c[h] * lax.rsqrt(jnp.sum(k_c[h] * k_c[h], axis=-1, keepdims=True) + EPS) for h in heads]
    kbs = [ks[h] * betas[h] for h in heads]
    kbf = [ks[h].astype(BF16) for h in heads]
    a_kk = [jnp.where(strict, _dot_nt(kbs[h].astype(BF16), kbf[h]) * decays[h], 0.0) for h in heads]
    a_qk = [(_dot_nt(qs[h].astype(BF16), kbf[h]) * decays[h]).astype(BF16) for h in heads]
    ps = [eye - a_kk[h] for h in heads]
    if levels > 1:
        xb = [(-a_kk[h]).astype(BF16) for h in heads]
        yj = [_dot(xb[h], xb[h]) for h in heads]
        for j in range(1, levels):
            yb = [yj[h].astype(BF16) for h in heads]
            if j < levels - 1:
                r = [_dot(yb[h], jnp.concatenate([ps[h].astype(BF16), yb[h]], axis=1)) for h in heads]
                ps = [ps[h] + r[h][:, :c] for h in heads]
                yj = [r[h][:, c:] for h in heads]
            else:
                ps = [ps[h] + _dot(yb[h], ps[h].astype(BF16)) for h in heads]
    egc = [jnp.exp(gcs[h]) for h in heads]
    uw = [_dot(ps[h].astype(BF16), jnp.concatenate([v_c[h] * betas[h], kbs[h] * egc[h]], axis=1).astype(BF16))
          for h in heads]
    lhs = [expand(jnp.concatenate([uw[h][:, hv:], qs[h] * egc[h]], axis=0), rseq2 if nseq > 1 else None).astype(BF16)
           for h in heads]
    ws = [_dot(lhs[h], s_in[h].astype(BF16)) for h in heads]
    v_new = [(uw[h][:, :hv] - ws[h][:c]).astype(BF16) for h in heads]
    o = [ws[h][c:] + _dot(a_qk[h], v_new[h]) for h in heads]
    kd = [expand(ks[h] * jnp.exp(gls[h] - gcs[h]), rseq if nseq > 1 else None).astype(BF16) for h in heads]
    upd = [_dot_tn(kd[h], v_new[h]) for h in heads]
    if nseq == 1:
        sdec = [jnp.exp(gls[h][0:1, :]) for h in heads]
    else:
        sdec = [jnp.concatenate([jnp.broadcast_to(jnp.exp(gls[h][b * tc:b * tc + 1, :]), (hk, 1))
                                 for b in range(nseq)], axis=0) for h in heads]
    s_out = [s_in[h] * sdec[h] + upd[h] for h in heads]
    y_out = [_rms(o[h], on_ref[...]) * _silu(z_in[h]) for h in heads]
    for h in heads:
        s_ref[h] = s_out[h]
        y_ref[:, h * hv:(h + 1) * hv] = y_out[h]

    if n_steps > 1:
        ext_ref[:, halo0:CONV_HALO, :] = ext_ref[:, halo0 + tc:CONV_HALO + tc, :]

    @pl.when(n == n_steps - 1)
    def _():
        for h in range(nh):
            for b in range(nseq):
                so_ref[b, h] = s_ref[h, b * hk:(b + 1) * hk, :]


def delta_mixer(proj, ab, row0, b, t, col_q, conv_state, rec_state, layer, conv_w, a_log, dt_bias, o_norm,
                rec_out, out_layer):
    _, _, nh, hk, hv = rec_state.shape
    dq, dv = nh * hk, nh * hv
    taps = conv_w.shape[0]
    assert dq == dv and col_q % dq == 0 and nh <= 8 and taps - 1 <= CONV_HALO
    if t >= CHUNK:
        nseq, tc = 1, CHUNK
    else:
        tc = t
        nseq = CHUNK // tc
    assert t % tc == 0 and b % nseq == 0 and tc % 8 == 0
    c = nseq * tc
    n_steps = t // tc
    assert row0 % c == 0
    rb0 = row0 // c
    cq = col_q // dq
    pad = lambda v: jnp.zeros((1, LANES), F32).at[0, :nh].set(v.astype(F32))
    body = functools.partial(_delta_kernel, nseq=nseq, tc=tc, nh=nh, hk=hk, hv=hv, n_steps=n_steps, taps=taps)
    row_map = lambda col: (lambda i, n: (rb0 + i * n_steps + n, col))
    n_in = 11
    n_layers = rec_out.shape[0]
    extra_specs, extra_args, aliases = [pl.BlockSpec(memory_space=pl.ANY)], [rec_out], {n_in: 1}
    kern = lambda *refs: body(*refs[:n_in], *refs[n_in + 1:])
    y, s_out = pl.pallas_call(
        kern,
        grid=(b // nseq, n_steps),
        input_output_aliases=aliases,
        in_specs=[
            pl.BlockSpec((c, dq), row_map(cq)),
            pl.BlockSpec((c, dq), row_map(cq + 1)),
            pl.BlockSpec((c, dv), row_map(cq + 2)),
            pl.BlockSpec((c, dv), row_map(cq + 3)),
            pl.BlockSpec((c, LANES), row_map(0)),
            pl.BlockSpec((nseq, taps - 1, 2 * dq + dv), lambda i, n: (i, 0, 0)),
            pl.BlockSpec((None, nseq, nh, hk, hv), lambda i, n: (layer, i, 0, 0, 0)),
            pl.BlockSpec((taps, 2 * dq + dv), lambda i, n: (0, 0)),
            pl.BlockSpec((1, LANES), lambda i, n: (0, 0)),
            pl.BlockSpec((1, LANES), lambda i, n: (0, 0)),
            pl.BlockSpec((1, hv), lambda i, n: (0, 0)),
        ] + extra_specs,
        out_specs=[
            pl.BlockSpec((c, dv), lambda i, n: (i * n_steps + n, 0)),
            pl.BlockSpec((None, nseq, nh, hk, hv), lambda i, n: (out_layer, i, 0, 0, 0)),
        ],
        out_shape=[jax.ShapeDtypeStruct((b * t, dv), F32), jax.ShapeDtypeStruct((n_layers, b, nh, hk, hv), F32)],
        scratch_shapes=[
            pltpu.VMEM((nseq, tc + CONV_HALO, 2 * dq + dv), F32),
            pltpu.VMEM((nh, nseq * hk, hv), F32),
        ],
        compiler_params=_params(2),
        name="delta_mixer",
    )(proj, proj, proj, proj, ab, conv_state, rec_state, conv_w, pad(a_log), pad(dt_bias), o_norm.reshape(1, hv),
      *extra_args)
    return y, s_out


def _merge_kernel(ya0_ref, ya1_ref, yb0_ref, yb1_ref, wa_ref, wb_ref, ga_ref, gb_ref, o_ref, wab_ref, wbb_ref,
                  *, n_first):
    i = pl.program_id(1)

    @pl.when(i == 0)
    def _():
        wab_ref[...] = wa_ref[...].astype(BF16)
        wbb_ref[...] = wb_ref[...].astype(BF16)

    first = i < n_first
    y_a = jnp.where(first, ya0_ref[...], ya1_ref[...])
    y_b = jnp.where(first, yb0_ref[...], yb1_ref[...])
    a = _dot(y_a.astype(BF16), wab_ref[...])
    b = _dot(y_b.astype(BF16), wbb_ref[...])
    o_ref[...] = (jax.nn.sigmoid(ga_ref[...]) * a + jax.nn.sigmoid(gb_ref[...]) * b).astype(o_ref.dtype)


def merge_branches(y_a, y_b, w_a, w_b, layer, gates):
    dp, dv = y_a[0].shape[1], y_b[0].shape[1]
    m = y_a[0].shape[0] + y_a[1].shape[0]
    d = w_a.shape[-1]
    tm = _pick(int(np.gcd(y_a[0].shape[0], y_a[1].shape[0])), 512, 16)
    tn = _pick(d, 1024, LANES)
    nj = d // tn
    n_first, a_specs = _two_source_specs(y_a[0], y_a[1], tm, lambda j, i: i)
    _, b_specs = _two_source_specs(y_b[0], y_b[1], tm, lambda j, i: i)
    return pl.pallas_call(
        functools.partial(_merge_kernel, n_first=n_first),
        grid=(nj, m // tm),
        in_specs=a_specs + b_specs + [
            pl.BlockSpec((None, dp, tn), lambda j, i: (layer, 0, j)),
            pl.BlockSpec((None, dv, tn), lambda j, i: (layer, 0, j)),
            pl.BlockSpec((tm, tn), lambda j, i: (i, j)),
            pl.BlockSpec((tm, tn), lambda j, i: (i, nj + j)),
        ],
        out_specs=pl.BlockSpec((tm, tn), lambda j, i: (i, j)),
        out_shape=jax.ShapeDtypeStruct((m, d), BF16),
        scratch_shapes=[pltpu.VMEM((dp, tn), BF16), pltpu.VMEM((dv, tn), BF16)],
        compiler_params=_params(2),
        name="merge_branches",
    )(*y_a, *y_b, w_a, w_b, gates, gates)


def _outproj_kernel(m_ref, w_ref, nw_ref, *rest, n_a):
    if n_a is None:
        h_ref, ho_ref, f_ref = rest
        h = h_ref[...]
    else:
        ha_ref, hb_ref, ho_ref, f_ref = rest
        h = jnp.where(pl.program_id(0) < n_a, ha_ref[...], hb_ref[...])
    hn = h + _dot(m_ref[...], w_ref[...])
    ho_ref[...] = hn
    f_ref[...] = _rms(hn, nw_ref[...]).astype(f_ref.dtype)


def out_proj(merged, w_out, h, norm_w):
    m, d = merged.shape
    row = lambda tm: pl.BlockSpec((tm, d), lambda i: (i, 0))
    if isinstance(h, tuple):
        tm = _pick(int(np.gcd(h[0].shape[0], h[1].shape[0])), 256, 16)
        n_a, h_specs = _two_source_specs(h[0], h[1], tm, lambda i: i)
        h_args = list(h)
    else:
        tm = _pick(m, 256, 16)
        n_a, h_specs, h_args = None, [row(tm)], [h]
    return pl.pallas_call(
        functools.partial(_outproj_kernel, n_a=n_a),
        grid=(m // tm,),
        in_specs=[row(tm), pl.BlockSpec((d, d), lambda i: (0, 0)), pl.BlockSpec((1, d), lambda i: (0, 0))] + h_specs,
        out_specs=[row(tm), row(tm)],
        out_shape=[jax.ShapeDtypeStruct((m, d), F32), jax.ShapeDtypeStruct((m, d), BF16)],
        compiler_params=_params(1),
        name="out_proj",
    )(merged, w_out, norm_w.reshape(1, d), *h_args)


def _merge_out_kernel(ya0_ref, ya1_ref, yb0_ref, yb1_ref, g_ref, wa_ref, wb_ref, wo_ref, nw_ref, *rest, n_first, two_h):
    if two_h:
        h0_ref, h1_ref, ho_ref, f_ref = rest
    else:
        h_ref, ho_ref, f_ref = rest
    first = pl.program_id(0) < n_first
    d = wo_ref.shape[0]
    y_a = jnp.where(first, ya0_ref[...], ya1_ref[...]).astype(BF16)
    y_b = jnp.where(first, yb0_ref[...], yb1_ref[...]).astype(BF16)
    merged = (jax.nn.sigmoid(g_ref[:, 0:d]) * _dot(y_a, wa_ref[...])
              + jax.nn.sigmoid(g_ref[:, d:2 * d]) * _dot(y_b, wb_ref[...])).astype(BF16)
    h = jnp.where(first, h0_ref[...], h1_ref[...]) if two_h else h_ref[...]
    hn = h + _dot(merged, wo_ref[...])
    ho_ref[...] = hn
    f_ref[...] = _rms(hn, nw_ref[...]).astype(f_ref.dtype)


def merge_out_proj(y_a, y_b, gates, w_a, w_b, w_out, h, norm_w):
    dp, dv = y_a[0].shape[1], y_b[0].shape[1]
    m, d = gates.shape[0], w_out.shape[0]
    tm = _pick(int(np.gcd(y_a[0].shape[0], y_a[1].shape[0])), 256, 16)
    n_first, a_specs = _two_source_specs(y_a[0], y_a[1], tm, lambda i: i)
    _, b_specs = _two_source_specs(y_b[0], y_b[1], tm, lambda i: i)
    two_h = isinstance(h, tuple)
    row = pl.BlockSpec((tm, d), lambda i: (i, 0))
    h_specs = _two_source_specs(h[0], h[1], tm, lambda i: i)[1] if two_h else [row]
    h_args = list(h) if two_h else [h]
    resident = lambda shape: pl.BlockSpec(shape, lambda i: (0, 0), pipeline_mode=pl.Buffered(1))
    return pl.pallas_call(
        functools.partial(_merge_out_kernel, n_first=n_first, two_h=two_h),
        grid=(m // tm,),
        in_specs=a_specs + b_specs + [
            pl.BlockSpec((tm, 2 * d), lambda i: (i, 0)),
            resident((dp, d)), resident((dv, d)), resident((d, d)),
            pl.BlockSpec((1, d), lambda i: (0, 0)),
        ] + h_specs,
        out_specs=[row, row],
        out_shape=[jax.ShapeDtypeStruct((m, d), F32), jax.ShapeDtypeStruct((m, d), BF16)],
        compiler_params=_params(1),
        name="merge_out_proj",
    )(*y_a, *y_b, gates, w_a, w_b, w_out, norm_w.reshape(1, d), *h_args)


def _mlp_kernel(se_ref, ns_ref, x_ref, wg_ref, wu_ref, wd_ref, o_ref, *, n_sub):
    s = pl.program_id(0)
    j = pl.program_id(1)
    n = ns_ref[s]
    n_pairs = n // 2
    odd = n - 2 * n_pairs

    def rows_of(block, n_blocks):
        return pl.ds(pl.multiple_of(block * SUB_ROWS, SUB_ROWS), n_blocks * SUB_ROWS)

    def update(rows, first):
        x = x_ref[rows, :]
        g = _dot(x, wg_ref[...].astype(BF16))
        u = _dot(x, wu_ref[...].astype(BF16))
        y = _dot((_silu(g) * u).astype(BF16), wd_ref[...].astype(BF16))
        if first:
            o_ref[rows, :] = y
        else:
            o_ref[rows, :] += y

    def sweep(first):
        def pair(p, carry):
            update(rows_of(2 * p, 2), first)
            return carry

        lax.fori_loop(0, n_pairs, pair, 0)

        @pl.when(odd == 1)
        def _():
            update(rows_of(2 * n_pairs, 1), first)

    @pl.when(j == 0)
    def _():
        sweep(True)

        def blank(r, carry):
            o_ref[rows_of(r, 1), :] = jnp.zeros((SUB_ROWS, o_ref.shape[1]), F32)
            return carry

        lax.fori_loop(n, n_sub, blank, 0)

    @pl.when(j > 0)
    def _():
        sweep(False)


def fused_mlp(x, w_gate, w_up, w_down, tile_expert, tile_sub, tm, tf):
    r, d = x.shape
    e, _, f = w_gate.shape
    n_tiles = r // tm
    n_sub = tm // SUB_ROWS
    nf = f // tf
    assert r % tm == 0 and tm % SUB_ROWS == 0 and f % tf == 0

    def wcol(s, j, se, ns):
        return (se[s], 0, jnp.where(ns[s] > 0, j, nf - 1))

    def wrow(s, j, se, ns):
        return (se[s], jnp.where(ns[s] > 0, j, nf - 1), 0)

    in_specs = [
        pl.BlockSpec((tm, d), lambda s, j, se, ns: (s, 0)),
        pl.BlockSpec((None, d, tf), wcol),
        pl.BlockSpec((None, d, tf), wcol),
        pl.BlockSpec((None, tf, d), wrow),
    ]
    return pl.pallas_call(
        functools.partial(_mlp_kernel, n_sub=n_sub),
        grid_spec=pltpu.PrefetchScalarGridSpec(
            num_scalar_prefetch=2,
            grid=(n_tiles, nf),
            in_specs=in_specs,
            out_specs=pl.BlockSpec((tm, d), lambda s, j, se, ns: (s, 0)),
        ),
        out_shape=jax.ShapeDtypeStruct((r, d), F32),
        compiler_params=_params(2),
        name="fused_mlp",
    )(tile_expert, tile_sub, x, w_gate, w_up, w_down)


def _router_kernel(h_ref, nw_ref, r_ref, f_ref, idx_ref, gate_ref, rank_ref, cnt_ref, seen_ref, *, n_exp):
    @pl.when(pl.program_id(0) == 0)
    def _():
        seen_ref[...] = jnp.zeros(seen_ref.shape, F32)

    f = _rms(h_ref[...], nw_ref[...])
    f_ref[...] = f
    f_hi = f.astype(BF16)
    f_lo = (f - f_hi.astype(F32)).astype(BF16)
    r = r_ref[...]
    r_hi = r.astype(BF16)
    r_lo = (r - r_hi.astype(F32)).astype(BF16)
    logits = _dot(f_hi, r_hi) + (_dot(f_lo, r_hi) + _dot(f_hi, r_lo))
    lane = lax.broadcasted_iota(jnp.int32, logits.shape, 1)
    m1 = jnp.max(logits, axis=-1, keepdims=True)
    i1 = jnp.min(jnp.where(logits == m1, lane, n_exp), axis=-1, keepdims=True)
    rest = jnp.where(lane == i1, -jnp.inf, logits)
    m2 = jnp.max(rest, axis=-1, keepdims=True)
    i2 = jnp.min(jnp.where(rest == m2, lane, n_exp), axis=-1, keepdims=True)
    ex = jnp.exp(m2 - m1)
    den = 1.0 + ex
    two = lax.broadcasted_iota(jnp.int32, idx_ref.shape, 1)
    idx_ref[...] = jnp.where(two == 0, i1, i2)
    gate_ref[...] = jnp.where(two == 0, 1.0 / den, ex / den)
    tm = logits.shape[0]
    chosen = (lane == i1) | (lane == i2)
    earlier = (lax.broadcasted_iota(jnp.int32, (tm, tm), 1) < lax.broadcasted_iota(jnp.int32, (tm, tm), 0))
    before = _dot(earlier.astype(BF16), chosen.astype(BF16)) + seen_ref[...]
    rank1 = jnp.sum(jnp.where(lane == i1, before, 0.0), axis=-1, keepdims=True)
    rank2 = jnp.sum(jnp.where(lane == i2, before, 0.0), axis=-1, keepdims=True)
    rank_ref[...] = jnp.where(two == 0, rank1, rank2).astype(jnp.int32)
    seen = seen_ref[...] + jnp.sum(chosen.astype(F32), axis=0, keepdims=True)
    seen_ref[...] = seen
    cnt_ref[...] = seen.astype(jnp.int32)


def router_top2(h, norm_w, router_w):
    m, d = h.shape
    n_exp = router_w.shape[1]
    tm = _pick(m, 512, 16)
    kern = functools.partial(_router_kernel, n_exp=n_exp)
    return pl.pallas_call(
        kern,
        grid=(m // tm,),
        in_specs=[
            pl.BlockSpec((tm, d), lambda i: (i, 0)),
            pl.BlockSpec((1, d), lambda i: (0, 0)),
            pl.BlockSpec((d, n_exp), lambda i: (0, 0)),
        ],
        out_specs=[
            pl.BlockSpec((tm, d), lambda i: (i, 0)),
            pl.BlockSpec((tm, TOP_K), lambda i: (i, 0)),
            pl.BlockSpec((tm, TOP_K), lambda i: (i, 0)),
            pl.BlockSpec((tm, TOP_K), lambda i: (i, 0)),
            pl.BlockSpec((1, n_exp), lambda i: (0, 0)),
        ],
        out_shape=[
            jax.ShapeDtypeStruct((m, d), F32),
            jax.ShapeDtypeStruct((m, TOP_K), jnp.int32),
            jax.ShapeDtypeStruct((m, TOP_K), F32),
            jax.ShapeDtypeStruct((m, TOP_K), jnp.int32),
            jax.ShapeDtypeStruct((1, n_exp), jnp.int32),
        ],
        scratch_shapes=[pltpu.VMEM((1, n_exp), F32)],
        compiler_params=_params(1),
        name="router_top2",
    )(h, norm_w.reshape(1, d), router_w)


def _gather_kernel(ns_ref, idx_ref, src_ref, dst_ref, buf, sem, *, n_sub):
    n = ns_ref[pl.program_id(0)]
    n_rows = n * SUB_ROWS

    def rows_of(b):
        return pl.ds(pl.multiple_of(b * SUB_ROWS, SUB_ROWS), SUB_ROWS)

    def row_copy(r):
        return pltpu.make_async_copy(src_ref.at[pl.ds(idx_ref[0, r], 1)], buf.at[pl.ds(r, 1)], sem)

    def start(g, carry):
        for u in range(DMA_UNROLL):
            row_copy(g * DMA_UNROLL + u).start()
        return carry

    def wait(g, carry):
        for u in range(DMA_UNROLL):
            row_copy(g * DMA_UNROLL + u).wait()
        return carry

    def blank(b, carry):
        dst_ref[rows_of(b), :] = jnp.zeros((SUB_ROWS, dst_ref.shape[1]), dst_ref.dtype)
        return carry

    def cast(b, carry):
        dst_ref[rows_of(b), :] = buf[rows_of(b), :].astype(dst_ref.dtype)
        return carry

    lax.fori_loop(0, n_rows // DMA_UNROLL, start, 0)
    lax.fori_loop(n, n_sub, blank, 0)
    lax.fori_loop(0, n_rows // DMA_UNROLL, wait, 0)
    lax.fori_loop(0, n, cast, 0)


def gather_rows(src, idx, tile_sub, tm, out_dtype=BF16):
    m, d = src.shape
    r = idx.shape[0]
    assert r % tm == 0 and src.dtype == F32
    kern = functools.partial(_gather_kernel, n_sub=tm // SUB_ROWS)
    return pl.pallas_call(
        kern,
        grid_spec=pltpu.PrefetchScalarGridSpec(
            num_scalar_prefetch=1,
            grid=(r // tm,),
            in_specs=[
                pl.BlockSpec((None, 1, tm), lambda s, ns: (s, 0, 0), memory_space=pltpu.SMEM),
                pl.BlockSpec(memory_space=pl.ANY),
            ],
            out_specs=pl.BlockSpec((tm, d), lambda s, ns: (s, 0)),
            scratch_shapes=[pltpu.VMEM((tm, d), F32), pltpu.SemaphoreType.DMA(())],
        ),
        out_shape=jax.ShapeDtypeStruct((r, d), out_dtype),
        compiler_params=_params(1),
        name="gather_rows",
    )(tile_sub, idx.reshape(r // tm, 1, tm), src)


def _combine_kernel(pos_ref, o_ref, g_ref, h_ref, nw_ref, ya_ref, yb_ref, buf, sem, *, tc, n_a):
    tokens_per_iter = DMA_UNROLL // TOP_K

    def row_copy(t, k):
        return pltpu.make_async_copy(o_ref.at[pl.ds(pos_ref[TOP_K * t + k], 1)], buf.at[k, pl.ds(t, 1)], sem)

    def start(g, carry):
        for u in range(tokens_per_iter):
            for k in range(TOP_K):
                row_copy(g * tokens_per_iter + u, k).start()
        return carry

    def wait(g, carry):
        for u in range(tokens_per_iter):
            for k in range(TOP_K):
                row_copy(g * tokens_per_iter + u, k).wait()
        return carry

    lax.fori_loop(0, tc // tokens_per_iter, start, 0)
    lax.fori_loop(0, tc // tokens_per_iter, wait, 0)
    g = g_ref[...]
    acc = h_ref[...]
    for k in range(TOP_K):
        acc = acc + g[:, k:k + 1] * buf[k]
    y = _rms(acc, nw_ref[...])
    i = pl.program_id(0)

    @pl.when(i < n_a)
    def _():
        ya_ref[...] = y

    @pl.when(i >= n_a)
    def _():
        yb_ref[...] = y


def combine_experts(expert_out, pos, gates, h, norm_w, rows_a):
    m, d = h.shape
    tc = _pick(np.gcd(rows_a, m - rows_a), 256, 8)
    n_a = rows_a // tc
    kern = functools.partial(_combine_kernel, tc=tc, n_a=n_a)
    return pl.pallas_call(
        kern,
        grid=(m // tc,),
        in_specs=[
            pl.BlockSpec((TOP_K * tc,), lambda i: (i,), memory_space=pltpu.SMEM),
            pl.BlockSpec(memory_space=pl.ANY),
            pl.BlockSpec((tc, TOP_K), lambda i: (i, 0)),
            pl.BlockSpec((tc, d), lambda i: (i, 0)),
            pl.BlockSpec((1, d), lambda i: (0, 0)),
        ],
        out_specs=[
            pl.BlockSpec((tc, d), lambda i: (jnp.minimum(i, n_a - 1), 0)),
            pl.BlockSpec((tc, d), lambda i: (jnp.maximum(i - n_a, 0), 0)),
        ],
        out_shape=[jax.ShapeDtypeStruct((rows_a, d), F32), jax.ShapeDtypeStruct((m - rows_a, d), F32)],
        scratch_shapes=[pltpu.VMEM((TOP_K, tc, d), F32), pltpu.SemaphoreType.DMA(())],
        compiler_params=_params(1),
        name="combine_experts",
    )(pos.reshape(-1), expert_out, gates, h, norm_w.reshape(1, d))


def routing_tables(top_i, rank, counts, tm):
    m = top_i.shape[0]
    n_exp = counts.shape[0]
    n_pairs = m * TOP_K
    n_tiles = n_pairs // tm + n_exp
    e_flat = top_i.reshape(-1)
    rank = rank.reshape(-1)
    tiles_e = (counts + tm - 1) // tm
    tile_end = jnp.cumsum(tiles_e)
    tile_start = tile_end - tiles_e
    dest = (tile_start * tm)[e_flat] + rank
    src_tok = jnp.zeros((n_tiles * tm,), jnp.int32).at[dest].set(jnp.arange(n_pairs, dtype=jnp.int32) // TOP_K)
    tile = jnp.arange(n_tiles, dtype=jnp.int32)
    used = tile < tile_end[-1]
    t_exp = jnp.minimum(jnp.sum((tile[:, None] >= tile_end[None, :]).astype(jnp.int32), axis=1), n_exp - 1)
    last_exp = t_exp[jnp.maximum(tile_end[-1] - 1, 0)]
    t_exp = jnp.where(used, t_exp, last_exp)
    rows_left = counts[t_exp] - (tile - tile_start[t_exp]) * tm
    t_sub = jnp.where(used, (jnp.clip(rows_left, 0, tm) + SUB_ROWS - 1) // SUB_ROWS, 0)
    return src_tok, dest.reshape(m, TOP_K), t_exp.astype(jnp.int32), t_sub.astype(jnp.int32)


def moe_layer(h, norm_w, router_w, w_gate, w_up, w_down, final_norm_w, rows_a):
    m, d = h.shape
    n_exp = router_w.shape[1]
    f = w_gate.shape[-1]
    tm = 6 * SUB_ROWS if m * TOP_K >= 8192 else SUB_ROWS
    tf = _pick(f, 256, LANES)
    f_norm, top_i, gates, rank, counts = router_top2(h, norm_w, router_w)
    src_tok, pos, t_exp, t_sub = routing_tables(top_i, rank, counts[0], tm)
    x_sorted = gather_rows(f_norm, src_tok, t_sub, tm)
    expert_out = fused_mlp(x_sorted, w_gate, w_up, w_down, t_exp, t_sub, tm, tf)
    return combine_experts(expert_out, pos, gates, h, final_norm_w, rows_a)


def _tail_rows(prev, proj, row0, t, col0):
    b, n, width = prev.shape
    k = min(n, t)
    if k == t or b * t <= 8 * b * k:
        new = lax.slice(proj, (row0, col0), (row0 + b * t, col0 + width)).reshape(b, t, width)[:, t - k:]
    else:
        new = jnp.stack([lax.slice(proj, (row0 + i * t + t - k, col0), (row0 + (i + 1) * t, col0 + width))
                         for i in range(b)])
    return new if k == n else jnp.concatenate([prev[:, t:], new], axis=1)


def dense_ffn(f_bf, w_gate, w_up, w_down):
    m, d = f_bf.shape
    f = w_gate.shape[-1]
    tm = _pick(m, 1024, 2 * SUB_ROWS)
    tf = _pick(f, 256, LANES)
    n_tiles = m // tm
    t_exp = jnp.zeros((n_tiles,), jnp.int32)
    t_sub = jnp.full((n_tiles,), tm // SUB_ROWS, jnp.int32)
    return fused_mlp(f_bf, w_gate[None], w_up[None], w_down[None], t_exp, t_sub, tm, tf)


def kernel(x_prompt, x_sample, state_pool, state_conv, state_rec, norm_mix, w_in, pool_w, pool_scale, conv_w,
           a_log, dt_bias, o_norm, w_branch_a, w_branch_b, w_out, norm_ffn, ffn_w_gate, ffn_w_up, ffn_w_down,
           router, moe_w_gate, moe_w_up, moe_w_down, norm_final):
    bp, tp, d = x_prompt.shape
    bs, ts, _ = x_sample.shape
    depth = w_in.shape[0]
    assert depth == 2, "layer 0 uses the dense FFN, layer 1 the expert FFN followed by the final norm"
    dp = state_pool.shape[-1]
    _, _, nh, hk, hv = state_rec.shape
    dq, dv = nh * hk, nh * hv
    taps = conv_w.shape[1]
    mp, ms = bp * tp, bs * ts
    n_main = dp + 2 * dq + 2 * dv
    col_ab = n_main
    col_gate = n_main + 2 * nh
    assert w_in.shape[-1] == col_gate + 2 * d

    x_pair = (x_prompt.reshape(mp, d), x_sample.reshape(ms, d))
    zeros_pool = jnp.zeros((bp,) + state_pool.shape[2:], F32)
    zeros_conv = jnp.zeros((bp,) + state_conv.shape[2:], F32)
    zeros_rec = jnp.zeros((1, bp) + state_rec.shape[2:], F32)
    pools_p, convs_p, pools_s, convs_s = [], [], [], []
    recs_p = jnp.zeros((depth, bp) + state_rec.shape[2:], F32)
    recs_s = jnp.zeros(state_rec.shape, F32)
    h = ffn_out = None
    for layer in range(depth):
        w_ab = jnp.pad(w_in[layer, :, col_ab:col_gate], ((0, 0), (0, LANES - 2 * nh)))
        if layer == 0:
            a = rms_cast_pair(*x_pair, norm_mix[layer])
        else:
            h, a = add_rms_cast(h, ffn_out, norm_mix[layer])
        proj = matmul(a, w_in, n_main, layer=layer)
        ab = matmul(a, w_ab, LANES)
        gates = matmul(a, w_in[layer, :, col_gate:], 2 * d)
        pw = pool_w[layer].astype(BF16)
        ya_p = pool_mixer(proj, 0, bp, tp, zeros_pool, pw, pool_scale[layer], 0)
        ya_s = pool_mixer(proj, mp, bs, ts, state_pool[layer], pw, pool_scale[layer], PAST_LEN)
        yb_p, recs_p = delta_mixer(proj, ab, 0, bp, tp, dp, zeros_conv, zeros_rec, 0, conv_w[layer], a_log[layer],
                                   dt_bias[layer], o_norm[layer], recs_p, layer)
        yb_s, recs_s = delta_mixer(proj, ab, mp, bs, ts, dp, state_conv[layer], state_rec, layer, conv_w[layer],
                                   a_log[layer], dt_bias[layer], o_norm[layer], recs_s, layer)
        h, f_bf = merge_out_proj((ya_p, ya_s), (yb_p, yb_s), gates, w_branch_a[layer].astype(BF16),
                                 w_branch_b[layer].astype(BF16), w_out[layer].astype(BF16),
                                 x_pair if layer == 0 else h, norm_ffn[layer])
        if layer == 0:
            ffn_out = dense_ffn(f_bf, ffn_w_gate[0], ffn_w_up[0], ffn_w_down[0])
        else:
            y_p, y_s = moe_layer(h, norm_ffn[layer], router[0], moe_w_gate[0], moe_w_up[0], moe_w_down[0],
                                 norm_final, mp)
        pools_p.append(_tail_rows(zeros_pool, proj, 0, tp, 0))
        pools_s.append(_tail_rows(state_pool[layer], proj, mp, ts, 0))
        convs_p.append(_tail_rows(zeros_conv, proj, 0, tp, dp))
        convs_s.append(_tail_rows(state_conv[layer], proj, mp, ts, dp))
    return (y_p.reshape(bp, tp, d), y_s.reshape(bs, ts, d),
            jnp.stack(pools_p), jnp.stack(convs_p), recs_p,
            jnp.stack(pools_s), jnp.stack(convs_s), recs_s)
```

```python
import functools

import jax
import jax.numpy as jnp
import numpy as np
from jax import lax
from jax.experimental import pallas as pl
from jax.experimental.pallas import tpu as pltpu

F32 = jnp.float32
BF16 = jnp.bfloat16
EPS = 1e-6
POOL_WINDOWS = (2, 4, 8, 16)
POOL_HALO = 16
CONV_HALO = 8
CHUNK = 64
TOP_K = 2
PAST_LEN = 16384
LANES = 128
SUB_ROWS = 256
DMA_UNROLL = 8
VMEM_BYTES = 56 * 1024 * 1024


def _params(n_axes, vmem=VMEM_BYTES):
    return pltpu.CompilerParams(dimension_semantics=("arbitrary",) * n_axes, vmem_limit_bytes=vmem)


def _pick(n, pref, mult):
    if n <= pref:
        return n
    for c in range(pref, 0, -1):
        if n % c == 0 and c % mult == 0:
            return c
    raise ValueError(f"no tile for {n} (pref {pref}, mult {mult})")


def _dot(a, b):
    return jnp.dot(a, b, preferred_element_type=F32)


def _dot_nt(a, b):
    return lax.dot_general(a, b, (((1,), (1,)), ((), ())), preferred_element_type=F32)


def _dot_tn(a, b):
    return lax.dot_general(a, b, (((0,), (0,)), ((), ())), preferred_element_type=F32)


def _split3(x):
    hi = x.astype(BF16)
    r = x - hi.astype(F32)
    mid = r.astype(BF16)
    lo = (r - mid.astype(F32)).astype(BF16)
    return hi, mid, lo


def _rms(x, w):
    ms = jnp.mean(x * x, axis=-1, keepdims=True)
    return x * lax.rsqrt(ms + EPS) * w


def _silu(x):
    return x * jax.nn.sigmoid(x)


def _two_source_specs(xa, xb, tm, index_args):
    n_a = xa.shape[0] // tm
    d = xa.shape[1]
    assert xa.shape[0] % tm == 0 and xb.shape[0] % tm == 0
    first = lambda *g: (jnp.minimum(index_args(*g), n_a - 1), 0)
    second = lambda *g: (jnp.maximum(index_args(*g) - n_a, 0), 0)
    return n_a, [pl.BlockSpec((tm, d), first), pl.BlockSpec((tm, d), second)]


def _rms_pair_kernel(xa_ref, xb_ref, w_ref, o_ref, *, n_a):
    x = jnp.where(pl.program_id(0) < n_a, xa_ref[...], xb_ref[...])
    o_ref[...] = _rms(x, w_ref[...]).astype(o_ref.dtype)


def rms_cast_pair(xa, xb, w):
    d = xa.shape[1]
    m = xa.shape[0] + xb.shape[0]
    tm = _pick(int(np.gcd(xa.shape[0], xb.shape[0])), 512, 16)
    n_a, specs = _two_source_specs(xa, xb, tm, lambda i: i)
    return pl.pallas_call(
        functools.partial(_rms_pair_kernel, n_a=n_a),
        grid=(m // tm,),
        in_specs=specs + [pl.BlockSpec((1, d), lambda i: (0, 0))],
        out_specs=pl.BlockSpec((tm, d), lambda i: (i, 0)),
        out_shape=jax.ShapeDtypeStruct((m, d), BF16),
        compiler_params=_params(1),
        name="rms_cast_pair",
    )(xa, xb, w.reshape(1, d))


def _add_rms_kernel(h_ref, y_ref, w_ref, ho_ref, o_ref):
    hn = h_ref[...] + y_ref[...]
    ho_ref[...] = hn
    o_ref[...] = _rms(hn, w_ref[...]).astype(o_ref.dtype)


def add_rms_cast(h, y, w):
    m, d = h.shape
    tm = _pick(m, 512, 16)
    row = pl.BlockSpec((tm, d), lambda i: (i, 0))
    return pl.pallas_call(
        _add_rms_kernel,
        grid=(m // tm,),
        in_specs=[row, row, pl.BlockSpec((1, d), lambda i: (0, 0))],
        out_specs=[row, row],
        out_shape=[jax.ShapeDtypeStruct((m, d), F32), jax.ShapeDtypeStruct((m, d), BF16)],
        compiler_params=_params(1),
        name="add_rms_cast",
    )(h, y, w.reshape(1, d))


def _mm_kernel(x_ref, w_ref, o_ref, wb_ref):
    @pl.when(pl.program_id(1) == 0)
    def _():
        wb_ref[...] = w_ref[...].T.astype(BF16)

    o_ref[...] = _dot(x_ref[...], wb_ref[...])


def matmul(x, w_t, n, layer=None, tm_pref=1024, tn_pref=1024):
    m, k = x.shape
    tm = _pick(m, tm_pref, 16)
    tn = _pick(n, tn_pref, LANES)
    assert w_t.shape[-2] >= n and w_t.shape[-1] == k and w_t.dtype == F32
    if layer is None:
        w_spec = pl.BlockSpec((tn, k), lambda j, i: (j, 0))
    else:
        w_spec = pl.BlockSpec((None, tn, k), lambda j, i: (layer, j, 0))
    return pl.pallas_call(
        _mm_kernel,
        grid=(n // tn, m // tm),
        in_specs=[pl.BlockSpec((tm, k), lambda j, i: (i, 0)), w_spec],
        out_specs=pl.BlockSpec((tm, tn), lambda j, i: (i, j)),
        out_shape=jax.ShapeDtypeStruct((m, n), F32),
        scratch_shapes=[pltpu.VMEM((k, tn), BF16)],
        compiler_params=_params(2),
        name="in_proj",
    )(x, w_t)


def _pool_kernel(u_ref, st_ref, pw_ref, sc_ref, y_ref, ext_ref, *, nb, tt, pos0, gc, rc, n_steps):
    n = pl.program_id(1)
    dp = gc * len(POOL_WINDOWS)
    buf = POOL_HALO - 1

    @pl.when(n == 0)
    def _():
        ext_ref[:, 0:1, :] = jnp.zeros((nb, 1, dp), F32)
        ext_ref[:, 1:POOL_HALO, :] = st_ref[...]

    ext_ref[:, POOL_HALO:POOL_HALO + tt, :] = u_ref[...].reshape(nb, tt, dp)
    for c0 in range(0, tt, rc):
        row = lax.broadcasted_iota(jnp.int32, (1, rc, 1), 1) + (pos0 + c0) + n * tt
        for gi, w in enumerate(POOL_WINDOWS):
            lanes = slice(gi * gc, (gi + 1) * gc)
            base = POOL_HALO + c0
            cur = ext_ref[:, base:base + rc, lanes]
            acc = cur
            for i in range(1, w):
                acc = acc + ext_ref[:, base - i:base - i + rc, lanes]
            cnt = jnp.minimum(row + 1, w).astype(F32)
            d = acc / cnt - cur
            y = _dot(d.reshape(nb * rc, gc).astype(BF16), pw_ref[gi]) * sc_ref[:, lanes]
            if nb == 1:
                y_ref[c0:c0 + rc, lanes] = y
            else:
                y_ref[:, lanes] = y
    if n_steps > 1:
        ext_ref[:, 0:POOL_HALO, :] = ext_ref[:, tt:tt + POOL_HALO, :]
    del buf


def pool_mixer(proj, row0, b, t, state, pool_w, pool_scale, pos0):
    dp = state.shape[-1]
    gc = dp // len(POOL_WINDOWS)
    if t >= 512:
        nb, tt = 1, 512
    else:
        nb, tt = _pick(b, max(1, 256 // t), 1), t
    assert t % tt == 0 and b % nb == 0 and row0 % (nb * tt) == 0 and tt % 8 == 0
    assert nb == 1 or tt == t
    rc = min(tt, 256) if nb == 1 else tt
    n_steps = t // tt
    rb0 = row0 // (nb * tt)
    kern = functools.partial(_pool_kernel, nb=nb, tt=tt, pos0=pos0, gc=gc, rc=rc, n_steps=n_steps)
    return pl.pallas_call(
        kern,
        grid=(b // nb, n_steps),
        in_specs=[
            pl.BlockSpec((nb * tt, dp), lambda i, n: (rb0 + i * n_steps + n, 0)),
            pl.BlockSpec((nb, POOL_HALO - 1, dp), lambda i, n: (i, 0, 0)),
            pl.BlockSpec((len(POOL_WINDOWS), gc, gc), lambda i, n: (0, 0, 0)),
            pl.BlockSpec((1, dp), lambda i, n: (0, 0)),
        ],
        out_specs=pl.BlockSpec((nb * tt, dp), lambda i, n: (i * n_steps + n, 0)),
        out_shape=jax.ShapeDtypeStruct((b * t, dp), F32),
        scratch_shapes=[pltpu.VMEM((nb, tt + POOL_HALO, dp), F32)],
        compiler_params=_params(2),
        name="pool_mixer",
    )(proj, state, pool_w, pool_scale.reshape(1, dp))


def _delta_kernel(q_ref, k_ref, v_ref, z_ref, ab_ref, cs_ref, rs_ref, cw_ref, al_ref, dtb_ref, on_ref,
                  y_ref, so_ref, ext_ref, s_ref, *, nseq, tc, nh, hk, hv, n_steps, taps):
    n = pl.program_id(1)
    c = nseq * tc
    dq = nh * hk
    halo0 = CONV_HALO - (taps - 1)

    @pl.when(n == 0)
    def _():
        ext_ref[:, halo0:CONV_HALO, :] = cs_ref[...]
        for h in range(nh):
            for b in range(nseq):
                s_ref[h, b * hk:(b + 1) * hk, :] = rs_ref[b, h]

    ext_ref[:, CONV_HALO:CONV_HALO + tc, 0:dq] = q_ref[...].reshape(nseq, tc, dq)
    ext_ref[:, CONV_HALO:CONV_HALO + tc, dq:2 * dq] = k_ref[...].reshape(nseq, tc, dq)
    ext_ref[:, CONV_HALO:CONV_HALO + tc, 2 * dq:2 * dq + nh * hv] = v_ref[...].reshape(nseq, tc, nh * hv)

    def conv(col0, width):
        acc = ext_ref[:, halo0:halo0 + tc, col0:col0 + width] * cw_ref[0:1, col0:col0 + width]
        for j in range(1, taps):
            acc = acc + ext_ref[:, halo0 + j:halo0 + j + tc, col0:col0 + width] * cw_ref[j:j + 1, col0:col0 + width]
        return _silu(acc).reshape(c, width)

    ri = lax.broadcasted_iota(jnp.int32, (c, c), 0)
    ci = lax.broadcasted_iota(jnp.int32, (c, c), 1)
    same = (ri // tc) == (ci // tc)
    low = same & (ci <= ri)
    strict = same & (ci < ri)
    eye = (ri == ci).astype(F32)

    ab = ab_ref[...]
    g_full = -jnp.exp(al_ref[...]) * jax.nn.softplus(ab + dtb_ref[...])
    beta_full = jax.nn.sigmoid(ab)
    sel = jnp.concatenate([low.astype(BF16), same.astype(BF16)], axis=0)
    parts = _split3(g_full)
    sums = _dot(sel, parts[0]) + _dot(sel, parts[1]) + _dot(sel, parts[2])
    gcum_full, glast_full = sums[:c], sums[c:]
    pick = (lax.broadcasted_iota(jnp.int32, (8, LANES), 0) == lax.broadcasted_iota(jnp.int32, (8, LANES), 1)).astype(BF16)
    cparts = _split3(gcum_full)
    gcum_t = _dot_nt(pick, cparts[0]) + _dot_nt(pick, cparts[1]) + _dot_nt(pick, cparts[2])

    if nseq > 1:
        rseq = lax.broadcasted_iota(jnp.int32, (c, 1), 0) // tc
        rseq2 = lax.broadcasted_iota(jnp.int32, (2 * c, 1), 0) % c // tc

    def expand(x, seq_of_row):
        if nseq == 1:
            return x
        return jnp.concatenate([jnp.where(seq_of_row == b, x, 0.0) for b in range(nseq)], axis=1)

    levels = int(np.log2(tc))
    assert 2 ** levels == tc
    heads = range(nh)
    s_in = [s_ref[h] for h in heads]
    z_in = [z_ref[:, h * hv:(h + 1) * hv] for h in heads]
    q_c = [conv(h * hk, hk) for h in heads]
    k_c = [conv(dq + h * hk, hk) for h in heads]
    v_c = [conv(2 * dq + h * hv, hv) for h in heads]
    gcs = [gcum_full[:, h:h + 1] for h in heads]
    gls = [glast_full[:, h:h + 1] for h in heads]
    betas = [beta_full[:, nh + h:nh + h + 1] for h in heads]
    decays = [jnp.where(low, jnp.exp(jnp.where(low, gcs[h] - gcum_t[h:h + 1, :], 0.0)), 0.0) for h in heads]
    qs = [q_c[h] * lax.rsqrt(jnp.sum(q_c[h] * q_c[h], axis=-1, keepdims=True) + EPS) * (hk ** -0.5) for h in heads]
    ks = [k_c[h] * lax.rsqrt(jnp.sum(k_c[h] * k_c[h], axis=-1, keepdims=True) + EPS) for h in heads]
    kbs = [ks[h] * betas[h] for h in heads]
    kbf = [ks[h].astype(BF16) for h in heads]
    a_kk = [jnp.where(strict, _dot_nt(kbs[h].astype(BF16), kbf[h]) * decays[h], 0.0) for h in heads]
    a_qk = [(_dot_nt(qs[h].astype(BF16), kbf[h]) * decays[h]).astype(BF16) for h in heads]
    ps = [eye - a_kk[h] for h in heads]
    if levels > 1:
        xb = [(-a_kk[h]).astype(BF16) for h in heads]
        yj = [_dot(xb[h], xb[h]) for h in heads]
        for j in range(1, levels):
            yb = [yj[h].astype(BF16) for h in heads]
            if j < levels - 1:
                r = [_dot(yb[h], jnp.concatenate([ps[h].astype(BF16), yb[h]], axis=1)) for h in heads]
                ps = [ps[h] + r[h][:, :c] for h in heads]
                yj = [r[h][:, c:] for h in heads]
            else:
                ps = [ps[h] + _dot(yb[h], ps[h].astype(BF16)) for h in heads]
    egc = [jnp.exp(gcs[h]) for h in heads]
    uw = [_dot(ps[h].astype(BF16), jnp.concatenate([v_c[h] * betas[h], kbs[h] * egc[h]], axis=1).astype(BF16))
          for h in heads]
    lhs = [expand(jnp.concatenate([uw[h][:, hv:], qs[h] * egc[h]], axis=0), rseq2 if nseq > 1 else None).astype(BF16)
           for h in heads]
    ws = [_dot(lhs[h], s_in[h].astype(BF16)) for h in heads]
    v_new = [(uw[h][:, :hv] - ws[h][:c]).astype(BF16) for h in heads]
    o = [ws[h][c:] + _dot(a_qk[h], v_new[h]) for h in heads]
    kd = [expand(ks[h] * jnp.exp(gls[h] - gcs[h]), rseq if nseq > 1 else None).astype(BF16) for h in heads]
    upd = [_dot_tn(kd[h], v_new[h]) for h in heads]
    if nseq == 1:
        sdec = [jnp.exp(gls[h][0:1, :]) for h in heads]
    else:
        sdec = [jnp.concatenate([jnp.broadcast_to(jnp.exp(gls[h][b * tc:b * tc + 1, :]), (hk, 1))
                                 for b in range(nseq)], axis=0) for h in heads]
    s_out = [s_in[h] * sdec[h] + upd[h] for h in heads]
    y_out = [_rms(o[h], on_ref[...]) * _silu(z_in[h]) for h in heads]
    for h in heads:
        s_ref[h] = s_out[h]
        y_ref[:, h * hv:(h + 1) * hv] = y_out[h]

    if n_steps > 1:
        ext_ref[:, halo0:CONV_HALO, :] = ext_ref[:, halo0 + tc:CONV_HALO + tc, :]

    @pl.when(n == n_steps - 1)
    def _():
        for h in range(nh):
            for b in range(nseq):
                so_ref[b, h] = s_ref[h, b * hk:(b + 1) * hk, :]


def delta_mixer(proj, ab, row0, b, t, col_q, conv_state, rec_state, layer, conv_w, a_log, dt_bias, o_norm,
                rec_out, out_layer):
    _, _, nh, hk, hv = rec_state.shape
    dq, dv = nh * hk, nh * hv
    taps = conv_w.shape[0]
    assert dq == dv and col_q % dq == 0 and nh <= 8 and taps - 1 <= CONV_HALO
    if t >= CHUNK:
        nseq, tc = 1, CHUNK
    else:
        tc = t
        nseq = CHUNK // tc
    assert t % tc == 0 and b % nseq == 0 and tc % 8 == 0
    c = nseq * tc
    n_steps = t // tc
    assert row0 % c == 0
    rb0 = row0 // c
    cq = col_q // dq
    pad = lambda v: jnp.zeros((1, LANES), F32).at[0, :nh].set(v.astype(F32))
    body = functools.partial(_delta_kernel, nseq=nseq, tc=tc, nh=nh, hk=hk, hv=hv, n_steps=n_steps, taps=taps)
    row_map = lambda col: (lambda i, n: (rb0 + i * n_steps + n, col))
    n_in = 11
    n_layers = rec_out.shape[0]
    extra_specs, extra_args, aliases = [pl.BlockSpec(memory_space=pl.ANY)], [rec_out], {n_in: 1}
    kern = lambda *refs: body(*refs[:n_in], *refs[n_in + 1:])
    y, s_out = pl.pallas_call(
        kern,
        grid=(b // nseq, n_steps),
        input_output_aliases=aliases,
        in_specs=[
            pl.BlockSpec((c, dq), row_map(cq)),
            pl.BlockSpec((c, dq), row_map(cq + 1)),
            pl.BlockSpec((c, dv), row_map(cq + 2)),
            pl.BlockSpec((c, dv), row_map(cq + 3)),
            pl.BlockSpec((c, LANES), row_map(0)),
            pl.BlockSpec((nseq, taps - 1, 2 * dq + dv), lambda i, n: (i, 0, 0)),
            pl.BlockSpec((None, nseq, nh, hk, hv), lambda i, n: (layer, i, 0, 0, 0)),
            pl.BlockSpec((taps, 2 * dq + dv), lambda i, n: (0, 0)),
            pl.BlockSpec((1, LANES), lambda i, n: (0, 0)),
            pl.BlockSpec((1, LANES), lambda i, n: (0, 0)),
            pl.BlockSpec((1, hv), lambda i, n: (0, 0)),
        ] + extra_specs,
        out_specs=[
            pl.BlockSpec((c, dv), lambda i, n: (i * n_steps + n, 0)),
            pl.BlockSpec((None, nseq, nh, hk, hv), lambda i, n: (out_layer, i, 0, 0, 0)),
        ],
        out_shape=[jax.ShapeDtypeStruct((b * t, dv), F32), jax.ShapeDtypeStruct((n_layers, b, nh, hk, hv), F32)],
        scratch_shapes=[
            pltpu.VMEM((nseq, tc + CONV_HALO, 2 * dq + dv), F32),
            pltpu.VMEM((nh, nseq * hk, hv), F32),
        ],
        compiler_params=_params(2),
        name="delta_mixer",
    )(proj, proj, proj, proj, ab, conv_state, rec_state, conv_w, pad(a_log), pad(dt_bias), o_norm.reshape(1, hv),
      *extra_args)
    return y, s_out


def _merge_kernel(ya0_ref, ya1_ref, yb0_ref, yb1_ref, wa_ref, wb_ref, ga_ref, gb_ref, o_ref, wab_ref, wbb_ref,
                  *, n_first):
    i = pl.program_id(1)

    @pl.when(i == 0)
    def _():
        wab_ref[...] = wa_ref[...].astype(BF16)
        wbb_ref[...] = wb_ref[...].astype(BF16)

    first = i < n_first
    y_a = jnp.where(first, ya0_ref[...], ya1_ref[...])
    y_b = jnp.where(first, yb0_ref[...], yb1_ref[...])
    a = _dot(y_a.astype(BF16), wab_ref[...])
    b = _dot(y_b.astype(BF16), wbb_ref[...])
    o_ref[...] = (jax.nn.sigmoid(ga_ref[...]) * a + jax.nn.sigmoid(gb_ref[...]) * b).astype(o_ref.dtype)


def merge_branches(y_a, y_b, w_a, w_b, layer, gates):
    dp, dv = y_a[0].shape[1], y_b[0].shape[1]
    m = y_a[0].shape[0] + y_a[1].shape[0]
    d = w_a.shape[-1]
    tm = _pick(int(np.gcd(y_a[0].shape[0], y_a[1].shape[0])), 512, 16)
    tn = _pick(d, 1024, LANES)
    nj = d // tn
    n_first, a_specs = _two_source_specs(y_a[0], y_a[1], tm, lambda j, i: i)
    _, b_specs = _two_source_specs(y_b[0], y_b[1], tm, lambda j, i: i)
    return pl.pallas_call(
        functools.partial(_merge_kernel, n_first=n_first),
        grid=(nj, m // tm),
        in_specs=a_specs + b_specs + [
            pl.BlockSpec((None, dp, tn), lambda j, i: (layer, 0, j)),
            pl.BlockSpec((None, dv, tn), lambda j, i: (layer, 0, j)),
            pl.BlockSpec((tm, tn), lambda j, i: (i, j)),
            pl.BlockSpec((tm, tn), lambda j, i: (i, nj + j)),
        ],
        out_specs=pl.BlockSpec((tm, tn), lambda j, i: (i, j)),
        out_shape=jax.ShapeDtypeStruct((m, d), BF16),
        scratch_shapes=[pltpu.VMEM((dp, tn), BF16), pltpu.VMEM((dv, tn), BF16)],
        compiler_params=_params(2),
        name="merge_branches",
    )(*y_a, *y_b, w_a, w_b, gates, gates)


def _outproj_kernel(m_ref, w_ref, nw_ref, *rest, n_a):
    if n_a is None:
        h_ref, ho_ref, f_ref = rest
        h = h_ref[...]
    else:
        ha_ref, hb_ref, ho_ref, f_ref = rest
        h = jnp.where(pl.program_id(0) < n_a, ha_ref[...], hb_ref[...])
    hn = h + _dot(m_ref[...], w_ref[...])
    ho_ref[...] = hn
    f_ref[...] = _rms(hn, nw_ref[...]).astype(f_ref.dtype)


def out_proj(merged, w_out, h, norm_w):
    m, d = merged.shape
    row = lambda tm: pl.BlockSpec((tm, d), lambda i: (i, 0))
    if isinstance(h, tuple):
        tm = _pick(int(np.gcd(h[0].shape[0], h[1].shape[0])), 256, 16)
        n_a, h_specs = _two_source_specs(h[0], h[1], tm, lambda i: i)
        h_args = list(h)
    else:
        tm = _pick(m, 256, 16)
        n_a, h_specs, h_args = None, [row(tm)], [h]
    return pl.pallas_call(
        functools.partial(_outproj_kernel, n_a=n_a),
        grid=(m // tm,),
        in_specs=[row(tm), pl.BlockSpec((d, d), lambda i: (0, 0)), pl.BlockSpec((1, d), lambda i: (0, 0))] + h_specs,
        out_specs=[row(tm), row(tm)],
        out_shape=[jax.ShapeDtypeStruct((m, d), F32), jax.ShapeDtypeStruct((m, d), BF16)],
        compiler_params=_params(1),
        name="out_proj",
    )(merged, w_out, norm_w.reshape(1, d), *h_args)


def _merge_out_kernel(ya0_ref, ya1_ref, yb0_ref, yb1_ref, g_ref, wa_ref, wb_ref, wo_ref, nw_ref, *rest, n_first, two_h):
    if two_h:
        h0_ref, h1_ref, ho_ref, f_ref = rest
    else:
        h_ref, ho_ref, f_ref = rest
    first = pl.program_id(0) < n_first
    d = wo_ref.shape[0]
    y_a = jnp.where(first, ya0_ref[...], ya1_ref[...]).astype(BF16)
    y_b = jnp.where(first, yb0_ref[...], yb1_ref[...]).astype(BF16)
    merged = (jax.nn.sigmoid(g_ref[:, 0:d]) * _dot(y_a, wa_ref[...])
              + jax.nn.sigmoid(g_ref[:, d:2 * d]) * _dot(y_b, wb_ref[...])).astype(BF16)
    h = jnp.where(first, h0_ref[...], h1_ref[...]) if two_h else h_ref[...]
    hn = h + _dot(merged, wo_ref[...])
    ho_ref[...] = hn
    f_ref[...] = _rms(hn, nw_ref[...]).astype(f_ref.dtype)


def merge_out_proj(y_a, y_b, gates, w_a, w_b, w_out, h, norm_w):
    dp, dv = y_a[0].shape[1], y_b[0].shape[1]
    m, d = gates.shape[0], w_out.shape[0]
    tm = _pick(int(np.gcd(y_a[0].shape[0], y_a[1].shape[0])), 256, 16)
    n_first, a_specs = _two_source_specs(y_a[0], y_a[1], tm, lambda i: i)
    _, b_specs = _two_source_specs(y_b[0], y_b[1], tm, lambda i: i)
    two_h = isinstance(h, tuple)
    row = pl.BlockSpec((tm, d), lambda i: (i, 0))
    h_specs = _two_source_specs(h[0], h[1], tm, lambda i: i)[1] if two_h else [row]
    h_args = list(h) if two_h else [h]
    resident = lambda shape: pl.BlockSpec(shape, lambda i: (0, 0), pipeline_mode=pl.Buffered(1))
    return pl.pallas_call(
        functools.partial(_merge_out_kernel, n_first=n_first, two_h=two_h),
        grid=(m // tm,),
        in_specs=a_specs + b_specs + [
            pl.BlockSpec((tm, 2 * d), lambda i: (i, 0)),
            resident((dp, d)), resident((dv, d)), resident((d, d)),
            pl.BlockSpec((1, d), lambda i: (0, 0)),
        ] + h_specs,
        out_specs=[row, row],
        out_shape=[jax.ShapeDtypeStruct((m, d), F32), jax.ShapeDtypeStruct((m, d), BF16)],
        compiler_params=_params(1),
        name="merge_out_proj",
    )(*y_a, *y_b, gates, w_a, w_b, w_out, norm_w.reshape(1, d), *h_args)


def _mlp_kernel(se_ref, ns_ref, x_ref, wg_ref, wu_ref, wd_ref, o_ref, *, n_sub):
    s = pl.program_id(0)
    j = pl.program_id(1)
    n = ns_ref[s]
    n_pairs = n // 2
    odd = n - 2 * n_pairs

    def rows_of(block, n_blocks):
        return pl.ds(pl.multiple_of(block * SUB_ROWS, SUB_ROWS), n_blocks * SUB_ROWS)

    def update(rows, first):
        x = x_ref[rows, :]
        g = _dot(x, wg_ref[...].astype(BF16))
        u = _dot(x, wu_ref[...].astype(BF16))
        y = _dot((_silu(g) * u).astype(BF16), wd_ref[...].astype(BF16))
        if first:
            o_ref[rows, :] = y
        else:
            o_ref[rows, :] += y

    def sweep(first):
        @pl.when(n == n_sub)
        def _():
            update(pl.ds(0, n_sub * SUB_ROWS), first)

        @pl.when(n < n_sub)
        def _():
            def pair(p, carry):
                update(rows_of(2 * p, 2), first)
                return carry

            lax.fori_loop(0, n_pairs, pair, 0)

            @pl.when(odd == 1)
            def _():
                update(rows_of(2 * n_pairs, 1), first)

    @pl.when(j == 0)
    def _():
        sweep(True)

        def blank(r, carry):
            o_ref[rows_of(r, 1), :] = jnp.zeros((SUB_ROWS, o_ref.shape[1]), F32)
            return carry

        lax.fori_loop(n, n_sub, blank, 0)

    @pl.when(j > 0)
    def _():
        sweep(False)


def fused_mlp(x, w_gate, w_up, w_down, tile_expert, tile_sub, tm, tf):
    r, d = x.shape
    e, _, f = w_gate.shape
    n_tiles = r // tm
    n_sub = tm // SUB_ROWS
    nf = f // tf
    assert r % tm == 0 and tm % SUB_ROWS == 0 and f % tf == 0

    def wcol(s, j, se, ns):
        return (se[s], 0, jnp.where(ns[s] > 0, j, nf - 1))

    def wrow(s, j, se, ns):
        return (se[s], jnp.where(ns[s] > 0, j, nf - 1), 0)

    in_specs = [
        pl.BlockSpec((tm, d), lambda s, j, se, ns: (s, 0)),
        pl.BlockSpec((None, d, tf), wcol),
        pl.BlockSpec((None, d, tf), wcol),
        pl.BlockSpec((None, tf, d), wrow),
    ]
    return pl.pallas_call(
        functools.partial(_mlp_kernel, n_sub=n_sub),
        grid_spec=pltpu.PrefetchScalarGridSpec(
            num_scalar_prefetch=2,
            grid=(n_tiles, nf),
            in_specs=in_specs,
            out_specs=pl.BlockSpec((tm, d), lambda s, j, se, ns: (s, 0)),
        ),
        out_shape=jax.ShapeDtypeStruct((r, d), F32),
        compiler_params=_params(2),
        name="fused_mlp",
    )(tile_expert, tile_sub, x, w_gate, w_up, w_down)


def _router_kernel(h_ref, nw_ref, r_ref, f_ref, idx_ref, gate_ref, rank_ref, cnt_ref, seen_ref, *, n_exp):
    @pl.when(pl.program_id(0) == 0)
    def _():
        seen_ref[...] = jnp.zeros(seen_ref.shape, F32)

    f = _rms(h_ref[...], nw_ref[...])
    f_ref[...] = f
    f_hi = f.astype(BF16)
    f_lo = (f - f_hi.astype(F32)).astype(BF16)
    r = r_ref[...]
    r_hi = r.astype(BF16)
    r_lo = (r - r_hi.astype(F32)).astype(BF16)
    logits = _dot(f_hi, r_hi) + (_dot(f_lo, r_hi) + _dot(f_hi, r_lo))
    lane = lax.broadcasted_iota(jnp.int32, logits.shape, 1)
    m1 = jnp.max(logits, axis=-1, keepdims=True)
    i1 = jnp.min(jnp.where(logits == m1, lane, n_exp), axis=-1, keepdims=True)
    rest = jnp.where(lane == i1, -jnp.inf, logits)
    m2 = jnp.max(rest, axis=-1, keepdims=True)
    i2 = jnp.min(jnp.where(rest == m2, lane, n_exp), axis=-1, keepdims=True)
    ex = jnp.exp(m2 - m1)
    den = 1.0 + ex
    two = lax.broadcasted_iota(jnp.int32, idx_ref.shape, 1)
    idx_ref[...] = jnp.where(two == 0, i1, i2)
    gate_ref[...] = jnp.where(two == 0, 1.0 / den, ex / den)
    tm = logits.shape[0]
    chosen = (lane == i1) | (lane == i2)
    earlier = (lax.broadcasted_iota(jnp.int32, (tm, tm), 1) < lax.broadcasted_iota(jnp.int32, (tm, tm), 0))
    before = _dot(earlier.astype(BF16), chosen.astype(BF16)) + seen_ref[...]
    rank1 = jnp.sum(jnp.where(lane == i1, before, 0.0), axis=-1, keepdims=True)
    rank2 = jnp.sum(jnp.where(lane == i2, before, 0.0), axis=-1, keepdims=True)
    rank_ref[...] = jnp.where(two == 0, rank1, rank2).astype(jnp.int32)
    seen = seen_ref[...] + jnp.sum(chosen.astype(F32), axis=0, keepdims=True)
    seen_ref[...] = seen
    cnt_ref[...] = seen.astype(jnp.int32)


def router_top2(h, norm_w, router_w):
    m, d = h.shape
    n_exp = router_w.shape[1]
    tm = _pick(m, 512, 16)
    kern = functools.partial(_router_kernel, n_exp=n_exp)
    return pl.pallas_call(
        kern,
        grid=(m // tm,),
        in_specs=[
            pl.BlockSpec((tm, d), lambda i: (i, 0)),
            pl.BlockSpec((1, d), lambda i: (0, 0)),
            pl.BlockSpec((d, n_exp), lambda i: (0, 0)),
        ],
        out_specs=[
            pl.BlockSpec((tm, d), lambda i: (i, 0)),
            pl.BlockSpec((tm, TOP_K), lambda i: (i, 0)),
            pl.BlockSpec((tm, TOP_K), lambda i: (i, 0)),
            pl.BlockSpec((tm, TOP_K), lambda i: (i, 0)),
            pl.BlockSpec((1, n_exp), lambda i: (0, 0)),
        ],
        out_shape=[
            jax.ShapeDtypeStruct((m, d), F32),
            jax.ShapeDtypeStruct((m, TOP_K), jnp.int32),
            jax.ShapeDtypeStruct((m, TOP_K), F32),
            jax.ShapeDtypeStruct((m, TOP_K), jnp.int32),
            jax.ShapeDtypeStruct((1, n_exp), jnp.int32),
        ],
        scratch_shapes=[pltpu.VMEM((1, n_exp), F32)],
        compiler_params=_params(1),
        name="router_top2",
    )(h, norm_w.reshape(1, d), router_w)


def _gather_kernel(ns_ref, idx_ref, src_ref, dst_ref, buf, sem, *, n_sub):
    n = ns_ref[pl.program_id(0)]
    n_rows = n * SUB_ROWS

    def rows_of(b):
        return pl.ds(pl.multiple_of(b * SUB_ROWS, SUB_ROWS), SUB_ROWS)

    def row_copy(r):
        return pltpu.make_async_copy(src_ref.at[pl.ds(idx_ref[0, r], 1)], buf.at[pl.ds(r, 1)], sem)

    def start(g, carry):
        for u in range(DMA_UNROLL):
            row_copy(g * DMA_UNROLL + u).start()
        return carry

    def wait(g, carry):
        for u in range(DMA_UNROLL):
            row_copy(g * DMA_UNROLL + u).wait()
        return carry

    def blank(b, carry):
        dst_ref[rows_of(b), :] = jnp.zeros((SUB_ROWS, dst_ref.shape[1]), dst_ref.dtype)
        return carry

    def cast(b, carry):
        dst_ref[rows_of(b), :] = buf[rows_of(b), :].astype(dst_ref.dtype)
        return carry

    lax.fori_loop(0, n_rows // DMA_UNROLL, start, 0)
    lax.fori_loop(n, n_sub, blank, 0)
    lax.fori_loop(0, n_rows // DMA_UNROLL, wait, 0)
    lax.fori_loop(0, n, cast, 0)


def gather_rows(src, idx, tile_sub, tm, out_dtype=BF16):
    m, d = src.shape
    r = idx.shape[0]
    assert r % tm == 0 and src.dtype == F32
    kern = functools.partial(_gather_kernel, n_sub=tm // SUB_ROWS)
    return pl.pallas_call(
        kern,
        grid_spec=pltpu.PrefetchScalarGridSpec(
            num_scalar_prefetch=1,
            grid=(r // tm,),
            in_specs=[
                pl.BlockSpec((None, 1, tm), lambda s, ns: (s, 0, 0), memory_space=pltpu.SMEM),
                pl.BlockSpec(memory_space=pl.ANY),
            ],
            out_specs=pl.BlockSpec((tm, d), lambda s, ns: (s, 0)),
            scratch_shapes=[pltpu.VMEM((tm, d), F32), pltpu.SemaphoreType.DMA(())],
        ),
        out_shape=jax.ShapeDtypeStruct((r, d), out_dtype),
        compiler_params=_params(1),
        name="gather_rows",
    )(tile_sub, idx.reshape(r // tm, 1, tm), src)


def _combine_kernel(pos_ref, o_ref, g_ref, h_ref, nw_ref, ya_ref, yb_ref, buf, sem, *, tc, n_a):
    tokens_per_iter = DMA_UNROLL // TOP_K

    def row_copy(t, k):
        return pltpu.make_async_copy(o_ref.at[pl.ds(pos_ref[TOP_K * t + k], 1)], buf.at[k, pl.ds(t, 1)], sem)

    def start(g, carry):
        for u in range(tokens_per_iter):
            for k in range(TOP_K):
                row_copy(g * tokens_per_iter + u, k).start()
        return carry

    def wait(g, carry):
        for u in range(tokens_per_iter):
            for k in range(TOP_K):
                row_copy(g * tokens_per_iter + u, k).wait()
        return carry

    lax.fori_loop(0, tc // tokens_per_iter, start, 0)
    lax.fori_loop(0, tc // tokens_per_iter, wait, 0)
    g = g_ref[...]
    acc = h_ref[...]
    for k in range(TOP_K):
        acc = acc + g[:, k:k + 1] * buf[k]
    y = _rms(acc, nw_ref[...])
    i = pl.program_id(0)

    @pl.when(i < n_a)
    def _():
        ya_ref[...] = y

    @pl.when(i >= n_a)
    def _():
        yb_ref[...] = y


def combine_experts(expert_out, pos, gates, h, norm_w, rows_a):
    m, d = h.shape
    tc = _pick(np.gcd(rows_a, m - rows_a), 256, 8)
    n_a = rows_a // tc
    kern = functools.partial(_combine_kernel, tc=tc, n_a=n_a)
    return pl.pallas_call(
        kern,
        grid=(m // tc,),
        in_specs=[
            pl.BlockSpec((TOP_K * tc,), lambda i: (i,), memory_space=pltpu.SMEM),
            pl.BlockSpec(memory_space=pl.ANY),
            pl.BlockSpec((tc, TOP_K), lambda i: (i, 0)),
            pl.BlockSpec((tc, d), lambda i: (i, 0)),
            pl.BlockSpec((1, d), lambda i: (0, 0)),
        ],
        out_specs=[
            pl.BlockSpec((tc, d), lambda i: (jnp.minimum(i, n_a - 1), 0)),
            pl.BlockSpec((tc, d), lambda i: (jnp.maximum(i - n_a, 0), 0)),
        ],
        out_shape=[jax.ShapeDtypeStruct((rows_a, d), F32), jax.ShapeDtypeStruct((m - rows_a, d), F32)],
        scratch_shapes=[pltpu.VMEM((TOP_K, tc, d), F32), pltpu.SemaphoreType.DMA(())],
        compiler_params=_params(1),
        name="combine_experts",
    )(pos.reshape(-1), expert_out, gates, h, norm_w.reshape(1, d))


def routing_tables(top_i, rank, counts, tm):
    m = top_i.shape[0]
    n_exp = counts.shape[0]
    n_pairs = m * TOP_K
    n_tiles = n_pairs // tm + n_exp
    e_flat = top_i.reshape(-1)
    rank = rank.reshape(-1)
    tiles_e = (counts + tm - 1) // tm
    tile_end = jnp.cumsum(tiles_e)
    tile_start = tile_end - tiles_e
    dest = (tile_start * tm)[e_flat] + rank
    src_tok = jnp.zeros((n_tiles * tm,), jnp.int32).at[dest].set(jnp.arange(n_pairs, dtype=jnp.int32) // TOP_K)
    tile = jnp.arange(n_tiles, dtype=jnp.int32)
    used = tile < tile_end[-1]
    t_exp = jnp.minimum(jnp.sum((tile[:, None] >= tile_end[None, :]).astype(jnp.int32), axis=1), n_exp - 1)
    last_exp = t_exp[jnp.maximum(tile_end[-1] - 1, 0)]
    t_exp = jnp.where(used, t_exp, last_exp)
    rows_left = counts[t_exp] - (tile - tile_start[t_exp]) * tm
    t_sub = jnp.where(used, (jnp.clip(rows_left, 0, tm) + SUB_ROWS - 1) // SUB_ROWS, 0)
    return src_tok, dest.reshape(m, TOP_K), t_exp.astype(jnp.int32), t_sub.astype(jnp.int32)


def moe_layer(h, norm_w, router_w, w_gate, w_up, w_down, final_norm_w, rows_a):
    m, d = h.shape
    n_exp = router_w.shape[1]
    f = w_gate.shape[-1]
    tm = 6 * SUB_ROWS if m * TOP_K >= 8192 else SUB_ROWS
    tf = _pick(f, 256, LANES)
    f_norm, top_i, gates, rank, counts = router_top2(h, norm_w, router_w)
    src_tok, pos, t_exp, t_sub = routing_tables(top_i, rank, counts[0], tm)
    x_sorted = gather_rows(f_norm, src_tok, t_sub, tm)
    expert_out = fused_mlp(x_sorted, w_gate, w_up, w_down, t_exp, t_sub, tm, tf)
    return combine_experts(expert_out, pos, gates, h, final_norm_w, rows_a)


def _tail_rows(prev, proj, row0, t, col0):
    b, n, width = prev.shape
    k = min(n, t)
    if k == t or b * t <= 8 * b * k:
        new = lax.slice(proj, (row0, col0), (row0 + b * t, col0 + width)).reshape(b, t, width)[:, t - k:]
    else:
        new = jnp.stack([lax.slice(proj, (row0 + i * t + t - k, col0), (row0 + (i + 1) * t, col0 + width))
                         for i in range(b)])
    return new if k == n else jnp.concatenate([prev[:, t:], new], axis=1)


def dense_ffn(f_bf, w_gate, w_up, w_down):
    m, d = f_bf.shape
    f = w_gate.shape[-1]
    tm = _pick(m, 1024, 2 * SUB_ROWS)
    tf = _pick(f, 256, LANES)
    n_tiles = m // tm
    t_exp = jnp.zeros((n_tiles,), jnp.int32)
    t_sub = jnp.full((n_tiles,), tm // SUB_ROWS, jnp.int32)
    return fused_mlp(f_bf, w_gate[None], w_up[None], w_down[None], t_exp, t_sub, tm, tf)


def kernel(x_prompt, x_sample, state_pool, state_conv, state_rec, norm_mix, w_in, pool_w, pool_scale, conv_w,
           a_log, dt_bias, o_norm, w_branch_a, w_branch_b, w_out, norm_ffn, ffn_w_gate, ffn_w_up, ffn_w_down,
           router, moe_w_gate, moe_w_up, moe_w_down, norm_final):
    bp, tp, d = x_prompt.shape
    bs, ts, _ = x_sample.shape
    depth = w_in.shape[0]
    assert depth == 2, "layer 0 uses the dense FFN, layer 1 the expert FFN followed by the final norm"
    dp = state_pool.shape[-1]
    _, _, nh, hk, hv = state_rec.shape
    dq, dv = nh * hk, nh * hv
    taps = conv_w.shape[1]
    mp, ms = bp * tp, bs * ts
    n_main = dp + 2 * dq + 2 * dv
    col_ab = n_main
    col_gate = n_main + 2 * nh
    assert w_in.shape[-1] == col_gate + 2 * d

    x_pair = (x_prompt.reshape(mp, d), x_sample.reshape(ms, d))
    w_in_t = jnp.swapaxes(w_in, 1, 2)
    zeros_pool = jnp.zeros((bp,) + state_pool.shape[2:], F32)
    zeros_conv = jnp.zeros((bp,) + state_conv.shape[2:], F32)
    zeros_rec = jnp.zeros((1, bp) + state_rec.shape[2:], F32)
    pools_p, convs_p, pools_s, convs_s = [], [], [], []
    recs_p = jnp.zeros((depth, bp) + state_rec.shape[2:], F32)
    recs_s = jnp.zeros(state_rec.shape, F32)
    h = ffn_out = None
    for layer in range(depth):
        w_ab = jnp.pad(w_in_t[layer, col_ab:col_gate, :], ((0, LANES - 2 * nh), (0, 0)))
        if layer == 0:
            a = rms_cast_pair(*x_pair, norm_mix[layer])
        else:
            h, a = add_rms_cast(h, ffn_out, norm_mix[layer])
        proj = matmul(a, w_in_t, n_main, layer=layer)
        ab = matmul(a, w_ab, LANES)
        gates = matmul(a, w_in_t[layer, col_gate:, :], 2 * d)
        pw = pool_w[layer].astype(BF16)
        ya_p = pool_mixer(proj, 0, bp, tp, zeros_pool, pw, pool_scale[layer], 0)
        ya_s = pool_mixer(proj, mp, bs, ts, state_pool[layer], pw, pool_scale[layer], PAST_LEN)
        yb_p, recs_p = delta_mixer(proj, ab, 0, bp, tp, dp, zeros_conv, zeros_rec, 0, conv_w[layer], a_log[layer],
                                   dt_bias[layer], o_norm[layer], recs_p, layer)
        yb_s, recs_s = delta_mixer(proj, ab, mp, bs, ts, dp, state_conv[layer], state_rec, layer, conv_w[layer],
                                   a_log[layer], dt_bias[layer], o_norm[layer], recs_s, layer)
        h, f_bf = merge_out_proj((ya_p, ya_s), (yb_p, yb_s), gates, w_branch_a[layer].astype(BF16),
                                 w_branch_b[layer].astype(BF16), w_out[layer].astype(BF16),
                                 x_pair if layer == 0 else h, norm_ffn[layer])
        if layer == 0:
            ffn_out = dense_ffn(f_bf, ffn_w_gate[0], ffn_w_up[0], ffn_w_down[0])
        else:
            y_p, y_s = moe_layer(h, norm_ffn[layer], router[0], moe_w_gate[0], moe_w_up[0], moe_w_down[0],
                                 norm_final, mp)
        pools_p.append(_tail_rows(zeros_pool, proj, 0, tp, 0))
        pools_s.append(_tail_rows(state_pool[layer], proj, mp, ts, 0))
        convs_p.append(_tail_rows(zeros_conv, proj, 0, tp, dp))
        convs_s.append(_tail_rows(state_conv[layer], proj, mp, ts, dp))
    return (y_p.reshape(bp, tp, d), y_s.reshape(bs, ts, d),
            jnp.stack(pools_p), jnp.stack(convs_p), recs_p,
            jnp.stack(pools_s), jnp.stack(convs_s), recs_s)
```

```python
import functools

import jax
import jax.numpy as jnp
import numpy as np
from jax import lax
from jax.experimental import pallas as pl
from jax.experimental.pallas import tpu as pltpu

F32 = jnp.float32
BF16 = jnp.bfloat16
EPS = 1e-6
POOL_WINDOWS = (2, 4, 8, 16)
POOL_HALO = 16
CONV_HALO = 8
CHUNK = 64
TOP_K = 2
PAST_LEN = 16384
LANES = 128
SUB_ROWS = 256
DMA_UNROLL = 8
VMEM_BYTES = 56 * 1024 * 1024


def _params(n_axes, vmem=VMEM_BYTES):
    return pltpu.CompilerParams(dimension_semantics=("arbitrary",) * n_axes, vmem_limit_bytes=vmem)


def _pick(n, pref, mult):
    if n <= pref:
        return n
    for c in range(pref, 0, -1):
        if n % c == 0 and c % mult == 0:
            return c
    raise ValueError(f"no tile for {n} (pref {pref}, mult {mult})")


def _dot(a, b):
    return jnp.dot(a, b, preferred_element_type=F32)


def _dot_nt(a, b):
    return lax.dot_general(a, b, (((1,), (1,)), ((), ())), preferred_element_type=F32)


def _dot_tn(a, b):
    return lax.dot_general(a, b, (((0,), (0,)), ((), ())), preferred_element_type=F32)


def _split3(x):
    hi = x.astype(BF16)
    r = x - hi.astype(F32)
    mid = r.astype(BF16)
    lo = (r - mid.astype(F32)).astype(BF16)
    return hi, mid, lo


def _rms(x, w):
    ms = jnp.mean(x * x, axis=-1, keepdims=True)
    return x * lax.rsqrt(ms + EPS) * w


def _silu(x):
    return x * jax.nn.sigmoid(x)


def _two_source_specs(xa, xb, tm, index_args):
    n_a = xa.shape[0] // tm
    d = xa.shape[1]
    assert xa.shape[0] % tm == 0 and xb.shape[0] % tm == 0
    first = lambda *g: (jnp.minimum(index_args(*g), n_a - 1), 0)
    second = lambda *g: (jnp.maximum(index_args(*g) - n_a, 0), 0)
    return n_a, [pl.BlockSpec((tm, d), first), pl.BlockSpec((tm, d), second)]


def _rms_pair_kernel(xa_ref, xb_ref, w_ref, o_ref, *, n_a):
    x = jnp.where(pl.program_id(0) < n_a, xa_ref[...], xb_ref[...])
    o_ref[...] = _rms(x, w_ref[...]).astype(o_ref.dtype)


def rms_cast_pair(xa, xb, w):
    d = xa.shape[1]
    m = xa.shape[0] + xb.shape[0]
    tm = _pick(int(np.gcd(xa.shape[0], xb.shape[0])), 512, 16)
    n_a, specs = _two_source_specs(xa, xb, tm, lambda i: i)
    return pl.pallas_call(
        functools.partial(_rms_pair_kernel, n_a=n_a),
        grid=(m // tm,),
        in_specs=specs + [pl.BlockSpec((1, d), lambda i: (0, 0))],
        out_specs=pl.BlockSpec((tm, d), lambda i: (i, 0)),
        out_shape=jax.ShapeDtypeStruct((m, d), BF16),
        compiler_params=_params(1),
        name="rms_cast_pair",
    )(xa, xb, w.reshape(1, d))


def _add_rms_kernel(h_ref, y_ref, w_ref, ho_ref, o_ref):
    hn = h_ref[...] + y_ref[...]
    ho_ref[...] = hn
    o_ref[...] = _rms(hn, w_ref[...]).astype(o_ref.dtype)


def add_rms_cast(h, y, w):
    m, d = h.shape
    tm = _pick(m, 512, 16)
    row = pl.BlockSpec((tm, d), lambda i: (i, 0))
    return pl.pallas_call(
        _add_rms_kernel,
        grid=(m // tm,),
        in_specs=[row, row, pl.BlockSpec((1, d), lambda i: (0, 0))],
        out_specs=[row, row],
        out_shape=[jax.ShapeDtypeStruct((m, d), F32), jax.ShapeDtypeStruct((m, d), BF16)],
        compiler_params=_params(1),
        name="add_rms_cast",
    )(h, y, w.reshape(1, d))


def _mm_kernel(x_ref, w_ref, o_ref, wb_ref):
    @pl.when(pl.program_id(1) == 0)
    def _():
        wb_ref[...] = w_ref[...].T.astype(BF16)

    o_ref[...] = _dot(x_ref[...], wb_ref[...])


def matmul(x, w_t, n, layer=None, tm_pref=1024, tn_pref=1024):
    m, k = x.shape
    tm = _pick(m, tm_pref, 16)
    tn = _pick(n, tn_pref, LANES)
    assert w_t.shape[-2] >= n and w_t.shape[-1] == k and w_t.dtype == F32
    if layer is None:
        w_spec = pl.BlockSpec((tn, k), lambda j, i: (j, 0))
    else:
        w_spec = pl.BlockSpec((None, tn, k), lambda j, i: (layer, j, 0))
    return pl.pallas_call(
        _mm_kernel,
        grid=(n // tn, m // tm),
        in_specs=[pl.BlockSpec((tm, k), lambda j, i: (i, 0)), w_spec],
        out_specs=pl.BlockSpec((tm, tn), lambda j, i: (i, j)),
        out_shape=jax.ShapeDtypeStruct((m, n), F32),
        scratch_shapes=[pltpu.VMEM((k, tn), BF16)],
        compiler_params=_params(2),
        name="in_proj",
    )(x, w_t)


def _pool_kernel(u_ref, st_ref, pw_ref, sc_ref, y_ref, ext_ref, *, nb, tt, pos0, gc, rc, n_steps):
    n = pl.program_id(1)
    dp = gc * len(POOL_WINDOWS)

    @pl.when(n == 0)
    def _():
        ext_ref[:, 0:1, :] = jnp.zeros((nb, 1, dp), F32)
        ext_ref[:, 1:POOL_HALO, :] = st_ref[...]

    ext_ref[:, POOL_HALO:POOL_HALO + tt, :] = u_ref[...].reshape(nb, tt, dp)
    for c0 in range(0, tt, rc):
        row = lax.broadcasted_iota(jnp.int32, (1, rc, 1), 1) + (pos0 + c0) + n * tt
        for gi, w in enumerate(POOL_WINDOWS):
            lanes = slice(gi * gc, (gi + 1) * gc)
            base = POOL_HALO + c0
            cur = ext_ref[:, base:base + rc, lanes]
            acc = cur
            for i in range(1, w):
                acc = acc + ext_ref[:, base - i:base - i + rc, lanes]
            cnt = jnp.minimum(row + 1, w).astype(F32)
            d = acc / cnt - cur
            y = _dot(d.reshape(nb * rc, gc).astype(BF16), pw_ref[gi]) * sc_ref[:, lanes]
            if nb == 1:
                y_ref[c0:c0 + rc, lanes] = y
            else:
                y_ref[:, lanes] = y
    if n_steps > 1:
        ext_ref[:, 0:POOL_HALO, :] = ext_ref[:, tt:tt + POOL_HALO, :]


def pool_mixer(proj, row0, b, t, state, pool_w, pool_scale, pos0):
    dp = state.shape[-1]
    gc = dp // len(POOL_WINDOWS)
    if t >= 512:
        nb, tt = 1, 512
    else:
        nb, tt = _pick(b, max(1, 256 // t), 1), t
    assert t % tt == 0 and b % nb == 0 and row0 % (nb * tt) == 0 and tt % 8 == 0
    assert nb == 1 or tt == t
    rc = min(tt, 256) if nb == 1 else tt
    n_steps = t // tt
    rb0 = row0 // (nb * tt)
    kern = functools.partial(_pool_kernel, nb=nb, tt=tt, pos0=pos0, gc=gc, rc=rc, n_steps=n_steps)
    return pl.pallas_call(
        kern,
        grid=(b // nb, n_steps),
        in_specs=[
            pl.BlockSpec((nb * tt, dp), lambda i, n: (rb0 + i * n_steps + n, 0)),
            pl.BlockSpec((nb, POOL_HALO - 1, dp), lambda i, n: (i, 0, 0)),
            pl.BlockSpec((len(POOL_WINDOWS), gc, gc), lambda i, n: (0, 0, 0)),
            pl.BlockSpec((1, dp), lambda i, n: (0, 0)),
        ],
        out_specs=pl.BlockSpec((nb * tt, dp), lambda i, n: (i * n_steps + n, 0)),
        out_shape=jax.ShapeDtypeStruct((b * t, dp), F32),
        scratch_shapes=[pltpu.VMEM((nb, tt + POOL_HALO, dp), F32)],
        compiler_params=_params(2),
        name="pool_mixer",
    )(proj, state, pool_w, pool_scale.reshape(1, dp))


def _delta_kernel(q_ref, k_ref, v_ref, z_ref, ab_ref, cs_ref, rs_ref, cw_ref, al_ref, dtb_ref, on_ref,
                  y_ref, so_ref, ext_ref, s_ref, *, nseq, tc, nh, hk, hv, n_steps, taps):
    n = pl.program_id(1)
    c = nseq * tc
    dq = nh * hk
    halo0 = CONV_HALO - (taps - 1)

    @pl.when(n == 0)
    def _():
        ext_ref[:, halo0:CONV_HALO, :] = cs_ref[...]
        for h in range(nh):
            for b in range(nseq):
                s_ref[h, b * hk:(b + 1) * hk, :] = rs_ref[b, h]

    ext_ref[:, CONV_HALO:CONV_HALO + tc, 0:dq] = q_ref[...].reshape(nseq, tc, dq)
    ext_ref[:, CONV_HALO:CONV_HALO + tc, dq:2 * dq] = k_ref[...].reshape(nseq, tc, dq)
    ext_ref[:, CONV_HALO:CONV_HALO + tc, 2 * dq:2 * dq + nh * hv] = v_ref[...].reshape(nseq, tc, nh * hv)

    def conv(col0, width):
        acc = ext_ref[:, halo0:halo0 + tc, col0:col0 + width] * cw_ref[0:1, col0:col0 + width]
        for j in range(1, taps):
            acc = acc + ext_ref[:, halo0 + j:halo0 + j + tc, col0:col0 + width] * cw_ref[j:j + 1, col0:col0 + width]
        return _silu(acc).reshape(c, width)

    ri = lax.broadcasted_iota(jnp.int32, (c, c), 0)
    ci = lax.broadcasted_iota(jnp.int32, (c, c), 1)
    same = (ri // tc) == (ci // tc)
    low = same & (ci <= ri)
    strict = same & (ci < ri)
    eye = (ri == ci).astype(F32)

    ab = ab_ref[...]
    g_full = -jnp.exp(al_ref[...]) * jax.nn.softplus(ab + dtb_ref[...])
    beta_full = jax.nn.sigmoid(ab)
    parts = _split3(g_full)
    if nseq == 1:
        sel = low.astype(BF16)
        gcum_full = _dot(sel, parts[0]) + _dot(sel, parts[1]) + _dot(sel, parts[2])
        glast_full = jnp.broadcast_to(gcum_full[c - 1:c, :], (c, LANES))
    else:
        sel = jnp.concatenate([low.astype(BF16), same.astype(BF16)], axis=0)
        sums = _dot(sel, parts[0]) + _dot(sel, parts[1]) + _dot(sel, parts[2])
        gcum_full, glast_full = sums[:c], sums[c:]
    pick = (lax.broadcasted_iota(jnp.int32, (8, LANES), 0) == lax.broadcasted_iota(jnp.int32, (8, LANES), 1)).astype(BF16)
    cparts = _split3(gcum_full)
    gcum_t = _dot_nt(pick, cparts[0]) + _dot_nt(pick, cparts[1]) + _dot_nt(pick, cparts[2])

    if nseq > 1:
        rseq = lax.broadcasted_iota(jnp.int32, (c, 1), 0) // tc
        rseq2 = lax.broadcasted_iota(jnp.int32, (2 * c, 1), 0) % c // tc

    def expand(x, seq_of_row):
        if nseq == 1:
            return x
        return jnp.concatenate([jnp.where(seq_of_row == b, x, 0.0) for b in range(nseq)], axis=1)

    levels = int(np.log2(tc))
    assert 2 ** levels == tc
    heads = range(nh)
    s_in = [s_ref[h] for h in heads]
    z_in = [z_ref[:, h * hv:(h + 1) * hv] for h in heads]
    q_c = [conv(h * hk, hk) for h in heads]
    k_c = [conv(dq + h * hk, hk) for h in heads]
    v_c = [conv(2 * dq + h * hv, hv) for h in heads]
    gcs = [gcum_full[:, h:h + 1] for h in heads]
    gls = [glast_full[:, h:h + 1] for h in heads]
    betas = [beta_full[:, nh + h:nh + h + 1] for h in heads]
    decays = [jnp.where(low, jnp.exp(jnp.where(low, gcs[h] - gcum_t[h:h + 1, :], 0.0)), 0.0) for h in heads]
    qs = [q_c[h] * lax.rsqrt(jnp.sum(q_c[h] * q_c[h], axis=-1, keepdims=True) + EPS) * (hk ** -0.5) for h in heads]
    ks = [k_c[h] * lax.rsqrt(jnp.sum(k_c[h] * k_c[h], axis=-1, keepdims=True) + EPS) for h in heads]
    kbs = [ks[h] * betas[h] for h in heads]
    kbf = [ks[h].astype(BF16) for h in heads]
    a_kk = [jnp.where(strict, _dot_nt(kbs[h].astype(BF16), kbf[h]) * decays[h], 0.0) for h in heads]
    a_qk = [(_dot_nt(qs[h].astype(BF16), kbf[h]) * decays[h]).astype(BF16) for h in heads]
    ps = [eye - a_kk[h] for h in heads]
    if levels > 1:
        xb = [(-a_kk[h]).astype(BF16) for h in heads]
        yj = [_dot(xb[h], xb[h]) for h in heads]
        for j in range(1, levels):
            yb = [yj[h].astype(BF16) for h in heads]
            if j < levels - 1:
                r = [_dot(yb[h], jnp.concatenate([ps[h].astype(BF16), yb[h]], axis=1)) for h in heads]
                ps = [ps[h] + r[h][:, :c] for h in heads]
                yj = [r[h][:, c:] for h in heads]
            else:
                ps = [ps[h] + _dot(yb[h], ps[h].astype(BF16)) for h in heads]
    egc = [jnp.exp(gcs[h]) for h in heads]
    uw = [_dot(ps[h].astype(BF16), jnp.concatenate([v_c[h] * betas[h], kbs[h] * egc[h]], axis=1).astype(BF16))
          for h in heads]
    lhs = [expand(jnp.concatenate([uw[h][:, hv:], qs[h] * egc[h]], axis=0), rseq2 if nseq > 1 else None).astype(BF16)
           for h in heads]
    ws = [_dot(lhs[h], s_in[h].astype(BF16)) for h in heads]
    v_new = [(uw[h][:, :hv] - ws[h][:c]).astype(BF16) for h in heads]
    o = [ws[h][c:] + _dot(a_qk[h], v_new[h]) for h in heads]
    kd = [expand(ks[h] * jnp.exp(gls[h] - gcs[h]), rseq if nseq > 1 else None).astype(BF16) for h in heads]
    upd = [_dot_tn(kd[h], v_new[h]) for h in heads]
    if nseq == 1:
        sdec = [jnp.exp(gls[h][0:1, :]) for h in heads]
    else:
        sdec = [jnp.concatenate([jnp.broadcast_to(jnp.exp(gls[h][b * tc:b * tc + 1, :]), (hk, 1))
                                 for b in range(nseq)], axis=0) for h in heads]
    s_out = [s_in[h] * sdec[h] + upd[h] for h in heads]
    y_out = [_rms(o[h], on_ref[...]) * _silu(z_in[h]) for h in heads]
    for h in heads:
        s_ref[h] = s_out[h]
        y_ref[:, h * hv:(h + 1) * hv] = y_out[h]

    if n_steps > 1:
        ext_ref[:, halo0:CONV_HALO, :] = ext_ref[:, halo0 + tc:CONV_HALO + tc, :]

    @pl.when(n == n_steps - 1)
    def _():
        for h in range(nh):
            for b in range(nseq):
                so_ref[b, h] = s_ref[h, b * hk:(b + 1) * hk, :]


def delta_mixer(proj, ab, row0, b, t, col_q, conv_state, rec_state, layer, conv_w, a_log, dt_bias, o_norm,
                rec_out, out_layer):
    _, _, nh, hk, hv = rec_state.shape
    dq, dv = nh * hk, nh * hv
    taps = conv_w.shape[0]
    assert dq == dv and col_q % dq == 0 and nh <= 8 and taps - 1 <= CONV_HALO
    if t >= CHUNK:
        nseq, tc = 1, CHUNK
    else:
        tc = t
        nseq = CHUNK // tc
    assert t % tc == 0 and b % nseq == 0 and tc % 8 == 0
    c = nseq * tc
    n_steps = t // tc
    assert row0 % c == 0
    rb0 = row0 // c
    cq = col_q // dq
    pad = lambda v: jnp.zeros((1, LANES), F32).at[0, :nh].set(v.astype(F32))
    body = functools.partial(_delta_kernel, nseq=nseq, tc=tc, nh=nh, hk=hk, hv=hv, n_steps=n_steps, taps=taps)
    row_map = lambda col: (lambda i, n: (rb0 + i * n_steps + n, col))
    n_in = 11
    n_layers = rec_out.shape[0]
    extra_specs, extra_args, aliases = [pl.BlockSpec(memory_space=pl.ANY)], [rec_out], {n_in: 1}
    kern = lambda *refs: body(*refs[:n_in], *refs[n_in + 1:])
    y, s_out = pl.pallas_call(
        kern,
        grid=(b // nseq, n_steps),
        input_output_aliases=aliases,
        in_specs=[
            pl.BlockSpec((c, dq), row_map(cq)),
            pl.BlockSpec((c, dq), row_map(cq + 1)),
            pl.BlockSpec((c, dv), row_map(cq + 2)),
            pl.BlockSpec((c, dv), row_map(cq + 3)),
            pl.BlockSpec((c, LANES), row_map(0)),
            pl.BlockSpec((nseq, taps - 1, 2 * dq + dv), lambda i, n: (i, 0, 0)),
            pl.BlockSpec((None, nseq, nh, hk, hv), lambda i, n: (layer, i, 0, 0, 0)),
            pl.BlockSpec((taps, 2 * dq + dv), lambda i, n: (0, 0)),
            pl.BlockSpec((1, LANES), lambda i, n: (0, 0)),
            pl.BlockSpec((1, LANES), lambda i, n: (0, 0)),
            pl.BlockSpec((1, hv), lambda i, n: (0, 0)),
        ] + extra_specs,
        out_specs=[
            pl.BlockSpec((c, dv), lambda i, n: (i * n_steps + n, 0)),
            pl.BlockSpec((None, nseq, nh, hk, hv), lambda i, n: (out_layer, i, 0, 0, 0)),
        ],
        out_shape=[jax.ShapeDtypeStruct((b * t, dv), F32), jax.ShapeDtypeStruct((n_layers, b, nh, hk, hv), F32)],
        scratch_shapes=[
            pltpu.VMEM((nseq, tc + CONV_HALO, 2 * dq + dv), F32),
            pltpu.VMEM((nh, nseq * hk, hv), F32),
        ],
        compiler_params=_params(2),
        name="delta_mixer",
    )(proj, proj, proj, proj, ab, conv_state, rec_state, conv_w, pad(a_log), pad(dt_bias), o_norm.reshape(1, hv),
      *extra_args)
    return y, s_out


def _merge_out_kernel(ya0_ref, ya1_ref, yb0_ref, yb1_ref, g_ref, wa_ref, wb_ref, wo_ref, nw_ref, *rest, n_first, two_h):
    if two_h:
        h0_ref, h1_ref, ho_ref, f_ref = rest
    else:
        h_ref, ho_ref, f_ref = rest
    first = pl.program_id(0) < n_first
    d = wo_ref.shape[0]
    y_a = jnp.where(first, ya0_ref[...], ya1_ref[...]).astype(BF16)
    y_b = jnp.where(first, yb0_ref[...], yb1_ref[...]).astype(BF16)
    merged = (jax.nn.sigmoid(g_ref[:, 0:d]) * _dot(y_a, wa_ref[...])
              + jax.nn.sigmoid(g_ref[:, d:2 * d]) * _dot(y_b, wb_ref[...])).astype(BF16)
    h = jnp.where(first, h0_ref[...], h1_ref[...]) if two_h else h_ref[...]
    hn = h + _dot(merged, wo_ref[...])
    ho_ref[...] = hn
    f_ref[...] = _rms(hn, nw_ref[...]).astype(f_ref.dtype)


def merge_out_proj(y_a, y_b, gates, w_a, w_b, w_out, h, norm_w):
    dp, dv = y_a[0].shape[1], y_b[0].shape[1]
    m, d = gates.shape[0], w_out.shape[0]
    tm = _pick(int(np.gcd(y_a[0].shape[0], y_a[1].shape[0])), 256, 16)
    n_first, a_specs = _two_source_specs(y_a[0], y_a[1], tm, lambda i: i)
    _, b_specs = _two_source_specs(y_b[0], y_b[1], tm, lambda i: i)
    two_h = isinstance(h, tuple)
    row = pl.BlockSpec((tm, d), lambda i: (i, 0))
    h_specs = _two_source_specs(h[0], h[1], tm, lambda i: i)[1] if two_h else [row]
    h_args = list(h) if two_h else [h]
    resident = lambda shape: pl.BlockSpec(shape, lambda i: (0, 0), pipeline_mode=pl.Buffered(1))
    return pl.pallas_call(
        functools.partial(_merge_out_kernel, n_first=n_first, two_h=two_h),
        grid=(m // tm,),
        in_specs=a_specs + b_specs + [
            pl.BlockSpec((tm, 2 * d), lambda i: (i, 0)),
            resident((dp, d)), resident((dv, d)), resident((d, d)),
            pl.BlockSpec((1, d), lambda i: (0, 0)),
        ] + h_specs,
        out_specs=[row, row],
        out_shape=[jax.ShapeDtypeStruct((m, d), F32), jax.ShapeDtypeStruct((m, d), BF16)],
        compiler_params=_params(1),
        name="merge_out_proj",
    )(*y_a, *y_b, gates, w_a, w_b, w_out, norm_w.reshape(1, d), *h_args)


def _mlp_kernel(se_ref, ns_ref, x_ref, wg_ref, wu_ref, wd_ref, o_ref, *, n_sub):
    s = pl.program_id(0)
    j = pl.program_id(1)
    n = ns_ref[s]
    n_pairs = n // 2
    odd = n - 2 * n_pairs

    def rows_of(block, n_blocks):
        return pl.ds(pl.multiple_of(block * SUB_ROWS, SUB_ROWS), n_blocks * SUB_ROWS)

    def update(rows, first):
        x = x_ref[rows, :]
        g = _dot(x, wg_ref[...].astype(BF16))
        u = _dot(x, wu_ref[...].astype(BF16))
        y = _dot((_silu(g) * u).astype(BF16), wd_ref[...].astype(BF16))
        if first:
            o_ref[rows, :] = y
        else:
            o_ref[rows, :] += y

    def sweep(first):
        @pl.when(n == n_sub)
        def _():
            update(pl.ds(0, n_sub * SUB_ROWS), first)

        @pl.when(n < n_sub)
        def _():
            def pair(p, carry):
                update(rows_of(2 * p, 2), first)
                return carry

            lax.fori_loop(0, n_pairs, pair, 0)

            @pl.when(odd == 1)
            def _():
                update(rows_of(2 * n_pairs, 1), first)

    @pl.when(j == 0)
    def _():
        sweep(True)

        def blank(r, carry):
            o_ref[rows_of(r, 1), :] = jnp.zeros((SUB_ROWS, o_ref.shape[1]), F32)
            return carry

        lax.fori_loop(n, n_sub, blank, 0)

    @pl.when(j > 0)
    def _():
        sweep(False)


def fused_mlp(x, w_gate, w_up, w_down, tile_expert, tile_sub, tm, tf):
    r, d = x.shape
    e, _, f = w_gate.shape
    n_tiles = r // tm
    n_sub = tm // SUB_ROWS
    nf = f // tf
    assert r % tm == 0 and tm % SUB_ROWS == 0 and f % tf == 0

    def wcol(s, j, se, ns):
        return (se[s], 0, jnp.where(ns[s] > 0, j, nf - 1))

    def wrow(s, j, se, ns):
        return (se[s], jnp.where(ns[s] > 0, j, nf - 1), 0)

    in_specs = [
        pl.BlockSpec((tm, d), lambda s, j, se, ns: (s, 0)),
        pl.BlockSpec((None, d, tf), wcol),
        pl.BlockSpec((None, d, tf), wcol),
        pl.BlockSpec((None, tf, d), wrow),
    ]
    return pl.pallas_call(
        functools.partial(_mlp_kernel, n_sub=n_sub),
        grid_spec=pltpu.PrefetchScalarGridSpec(
            num_scalar_prefetch=2,
            grid=(n_tiles, nf),
            in_specs=in_specs,
            out_specs=pl.BlockSpec((tm, d), lambda s, j, se, ns: (s, 0)),
        ),
        out_shape=jax.ShapeDtypeStruct((r, d), F32),
        compiler_params=_params(2),
        name="fused_mlp",
    )(tile_expert, tile_sub, x, w_gate, w_up, w_down)


def _router_kernel(h_ref, nw_ref, r_ref, f_ref, idx_ref, gate_ref, rank_ref, cnt_ref, seen_ref, *, n_exp):
    @pl.when(pl.program_id(0) == 0)
    def _():
        seen_ref[...] = jnp.zeros(seen_ref.shape, F32)

    f = _rms(h_ref[...], nw_ref[...])
    f_ref[...] = f
    f_hi = f.astype(BF16)
    f_lo = (f - f_hi.astype(F32)).astype(BF16)
    r = r_ref[...]
    r_hi = r.astype(BF16)
    r_lo = (r - r_hi.astype(F32)).astype(BF16)
    logits = _dot(f_hi, r_hi) + (_dot(f_lo, r_hi) + _dot(f_hi, r_lo))
    lane = lax.broadcasted_iota(jnp.int32, logits.shape, 1)
    m1 = jnp.max(logits, axis=-1, keepdims=True)
    i1 = jnp.min(jnp.where(logits == m1, lane, n_exp), axis=-1, keepdims=True)
    rest = jnp.where(lane == i1, -jnp.inf, logits)
    m2 = jnp.max(rest, axis=-1, keepdims=True)
    i2 = jnp.min(jnp.where(rest == m2, lane, n_exp), axis=-1, keepdims=True)
    ex = jnp.exp(m2 - m1)
    den = 1.0 + ex
    two = lax.broadcasted_iota(jnp.int32, idx_ref.shape, 1)
    idx_ref[...] = jnp.where(two == 0, i1, i2)
    gate_ref[...] = jnp.where(two == 0, 1.0 / den, ex / den)
    tm = logits.shape[0]
    chosen = (lane == i1) | (lane == i2)
    earlier = (lax.broadcasted_iota(jnp.int32, (tm, tm), 1) < lax.broadcasted_iota(jnp.int32, (tm, tm), 0))
    before = _dot(earlier.astype(BF16), chosen.astype(BF16)) + seen_ref[...]
    rank1 = jnp.sum(jnp.where(lane == i1, before, 0.0), axis=-1, keepdims=True)
    rank2 = jnp.sum(jnp.where(lane == i2, before, 0.0), axis=-1, keepdims=True)
    rank_ref[...] = jnp.where(two == 0, rank1, rank2).astype(jnp.int32)
    seen = seen_ref[...] + jnp.sum(chosen.astype(F32), axis=0, keepdims=True)
    seen_ref[...] = seen
    cnt_ref[...] = seen.astype(jnp.int32)


def router_top2(h, norm_w, router_w):
    m, d = h.shape
    n_exp = router_w.shape[1]
    tm = _pick(m, 512, 16)
    kern = functools.partial(_router_kernel, n_exp=n_exp)
    return pl.pallas_call(
        kern,
        grid=(m // tm,),
        in_specs=[
            pl.BlockSpec((tm, d), lambda i: (i, 0)),
            pl.BlockSpec((1, d), lambda i: (0, 0)),
            pl.BlockSpec((d, n_exp), lambda i: (0, 0)),
        ],
        out_specs=[
            pl.BlockSpec((tm, d), lambda i: (i, 0)),
            pl.BlockSpec((tm, TOP_K), lambda i: (i, 0)),
            pl.BlockSpec((tm, TOP_K), lambda i: (i, 0)),
            pl.BlockSpec((tm, TOP_K), lambda i: (i, 0)),
            pl.BlockSpec((1, n_exp), lambda i: (0, 0)),
        ],
        out_shape=[
            jax.ShapeDtypeStruct((m, d), F32),
            jax.ShapeDtypeStruct((m, TOP_K), jnp.int32),
            jax.ShapeDtypeStruct((m, TOP_K), F32),
            jax.ShapeDtypeStruct((m, TOP_K), jnp.int32),
            jax.ShapeDtypeStruct((1, n_exp), jnp.int32),
        ],
        scratch_shapes=[pltpu.VMEM((1, n_exp), F32)],
        compiler_params=_params(1),
        name="router_top2",
    )(h, norm_w.reshape(1, d), router_w)


def _gather_kernel(ns_ref, idx_ref, src_ref, dst_ref, buf, sem, *, n_sub):
    n = ns_ref[pl.program_id(0)]
    n_rows = n * SUB_ROWS

    def rows_of(b):
        return pl.ds(pl.multiple_of(b * SUB_ROWS, SUB_ROWS), SUB_ROWS)

    def row_copy(r):
        return pltpu.make_async_copy(src_ref.at[pl.ds(idx_ref[0, r], 1)], buf.at[pl.ds(r, 1)], sem)

    def start(g, carry):
        for u in range(DMA_UNROLL):
            row_copy(g * DMA_UNROLL + u).start()
        return carry

    def wait(g, carry):
        for u in range(DMA_UNROLL):
            row_copy(g * DMA_UNROLL + u).wait()
        return carry

    def blank(b, carry):
        dst_ref[rows_of(b), :] = jnp.zeros((SUB_ROWS, dst_ref.shape[1]), dst_ref.dtype)
        return carry

    def cast(b, carry):
        dst_ref[rows_of(b), :] = buf[rows_of(b), :].astype(dst_ref.dtype)
        return carry

    lax.fori_loop(0, n_rows // DMA_UNROLL, start, 0)
    lax.fori_loop(n, n_sub, blank, 0)
    lax.fori_loop(0, n_rows // DMA_UNROLL, wait, 0)
    lax.fori_loop(0, n, cast, 0)


def gather_rows(src, idx, tile_sub, tm, out_dtype=BF16):
    m, d = src.shape
    r = idx.shape[0]
    assert r % tm == 0 and src.dtype == F32
    kern = functools.partial(_gather_kernel, n_sub=tm // SUB_ROWS)
    return pl.pallas_call(
        kern,
        grid_spec=pltpu.PrefetchScalarGridSpec(
            num_scalar_prefetch=1,
            grid=(r // tm,),
            in_specs=[
                pl.BlockSpec((None, 1, tm), lambda s, ns: (s, 0, 0), memory_space=pltpu.SMEM),
                pl.BlockSpec(memory_space=pl.ANY),
            ],
            out_specs=pl.BlockSpec((tm, d), lambda s, ns: (s, 0)),
            scratch_shapes=[pltpu.VMEM((tm, d), F32), pltpu.SemaphoreType.DMA(())],
        ),
        out_shape=jax.ShapeDtypeStruct((r, d), out_dtype),
        compiler_params=_params(1),
        name="gather_rows",
    )(tile_sub, idx.reshape(r // tm, 1, tm), src)


def _combine_kernel(pos_ref, o_ref, g_ref, h_ref, nw_ref, ya_ref, yb_ref, buf, sem, *, tc, n_a):
    def row_copy(r):
        return pltpu.make_async_copy(o_ref.at[pl.ds(pos_ref[r], 1)], buf.at[pl.ds(r, 1)], sem)

    def start(g, carry):
        for u in range(DMA_UNROLL):
            row_copy(g * DMA_UNROLL + u).start()
        return carry

    def wait(g, carry):
        for u in range(DMA_UNROLL):
            row_copy(g * DMA_UNROLL + u).wait()
        return carry

    lax.fori_loop(0, TOP_K * tc // DMA_UNROLL, start, 0)
    lax.fori_loop(0, TOP_K * tc // DMA_UNROLL, wait, 0)
    g = g_ref[...]
    acc = h_ref[...]
    for k in range(TOP_K):
        acc = acc + g[:, k:k + 1] * buf[k * tc:(k + 1) * tc, :]
    y = _rms(acc, nw_ref[...])
    i = pl.program_id(0)

    @pl.when(i < n_a)
    def _():
        ya_ref[...] = y

    @pl.when(i >= n_a)
    def _():
        yb_ref[...] = y


def combine_experts(expert_out, pos, gates, h, norm_w, rows_a):
    m, d = h.shape
    tc = _pick(np.gcd(rows_a, m - rows_a), 256, 8)
    n_a = rows_a // tc
    kern = functools.partial(_combine_kernel, tc=tc, n_a=n_a)
    return pl.pallas_call(
        kern,
        grid=(m // tc,),
        in_specs=[
            pl.BlockSpec((TOP_K * tc,), lambda i: (i,), memory_space=pltpu.SMEM),
            pl.BlockSpec(memory_space=pl.ANY),
            pl.BlockSpec((tc, TOP_K), lambda i: (i, 0)),
            pl.BlockSpec((tc, d), lambda i: (i, 0)),
            pl.BlockSpec((1, d), lambda i: (0, 0)),
        ],
        out_specs=[
            pl.BlockSpec((tc, d), lambda i: (jnp.minimum(i, n_a - 1), 0)),
            pl.BlockSpec((tc, d), lambda i: (jnp.maximum(i - n_a, 0), 0)),
        ],
        out_shape=[jax.ShapeDtypeStruct((rows_a, d), F32), jax.ShapeDtypeStruct((m - rows_a, d), F32)],
        scratch_shapes=[pltpu.VMEM((TOP_K * tc, d), F32), pltpu.SemaphoreType.DMA(())],
        compiler_params=_params(1),
        name="combine_experts",
    )(pos.reshape(m // tc, tc, TOP_K).transpose(0, 2, 1).reshape(-1), expert_out, gates, h, norm_w.reshape(1, d))


def routing_tables(top_i, rank, counts, tm):
    m = top_i.shape[0]
    n_exp = counts.shape[0]
    n_pairs = m * TOP_K
    n_tiles = n_pairs // tm + n_exp
    e_flat = top_i.reshape(-1)
    rank = rank.reshape(-1)
    tiles_e = (counts + tm - 1) // tm
    tile_end = jnp.cumsum(tiles_e)
    tile_start = tile_end - tiles_e
    dest = (tile_start * tm)[e_flat] + rank
    src_tok = jnp.zeros((n_tiles * tm,), jnp.int32).at[dest].set(jnp.arange(n_pairs, dtype=jnp.int32) // TOP_K)
    tile = jnp.arange(n_tiles, dtype=jnp.int32)
    used = tile < tile_end[-1]
    t_exp = jnp.minimum(jnp.sum((tile[:, None] >= tile_end[None, :]).astype(jnp.int32), axis=1), n_exp - 1)
    last_exp = t_exp[jnp.maximum(tile_end[-1] - 1, 0)]
    t_exp = jnp.where(used, t_exp, last_exp)
    rows_left = counts[t_exp] - (tile - tile_start[t_exp]) * tm
    t_sub = jnp.where(used, (jnp.clip(rows_left, 0, tm) + SUB_ROWS - 1) // SUB_ROWS, 0)
    return src_tok, dest.reshape(m, TOP_K), t_exp.astype(jnp.int32), t_sub.astype(jnp.int32)


def moe_layer(h, norm_w, router_w, w_gate, w_up, w_down, final_norm_w, rows_a):
    m, d = h.shape
    n_exp = router_w.shape[1]
    f = w_gate.shape[-1]
    tm = 6 * SUB_ROWS if m * TOP_K >= 8192 else SUB_ROWS
    tf = _pick(f, 256, LANES)
    f_norm, top_i, gates, rank, counts = router_top2(h, norm_w, router_w)
    src_tok, pos, t_exp, t_sub = routing_tables(top_i, rank, counts[0], tm)
    x_sorted = gather_rows(f_norm, src_tok, t_sub, tm)
    expert_out = fused_mlp(x_sorted, w_gate, w_up, w_down, t_exp, t_sub, tm, tf)
    return combine_experts(expert_out, pos, gates, h, final_norm_w, rows_a)


def _tail_rows(prev, proj, row0, t, col0):
    b, n, width = prev.shape
    k = min(n, t)
    if k == t or b * t <= 8 * b * k:
        new = lax.slice(proj, (row0, col0), (row0 + b * t, col0 + width)).reshape(b, t, width)[:, t - k:]
    else:
        new = jnp.stack([lax.slice(proj, (row0 + i * t + t - k, col0), (row0 + (i + 1) * t, col0 + width))
                         for i in range(b)])
    return new if k == n else jnp.concatenate([prev[:, t:], new], axis=1)


def dense_ffn(f_bf, w_gate, w_up, w_down):
    m, d = f_bf.shape
    f = w_gate.shape[-1]
    tm = _pick(m, 1024, 2 * SUB_ROWS)
    tf = _pick(f, 256, LANES)
    n_tiles = m // tm
    t_exp = jnp.zeros((n_tiles,), jnp.int32)
    t_sub = jnp.full((n_tiles,), tm // SUB_ROWS, jnp.int32)
    return fused_mlp(f_bf, w_gate[None], w_up[None], w_down[None], t_exp, t_sub, tm, tf)


def kernel(x_prompt, x_sample, state_pool, state_conv, state_rec, norm_mix, w_in, pool_w, pool_scale, conv_w,
           a_log, dt_bias, o_norm, w_branch_a, w_branch_b, w_out, norm_ffn, ffn_w_gate, ffn_w_up, ffn_w_down,
           router, moe_w_gate, moe_w_up, moe_w_down, norm_final):
    bp, tp, d = x_prompt.shape
    bs, ts, _ = x_sample.shape
    depth = w_in.shape[0]
    assert depth == 2, "layer 0 uses the dense FFN, layer 1 the expert FFN followed by the final norm"
    dp = state_pool.shape[-1]
    _, _, nh, hk, hv = state_rec.shape
    dq, dv = nh * hk, nh * hv
    mp, ms = bp * tp, bs * ts
    n_main = dp + 2 * dq + 2 * dv
    col_ab = n_main
    col_gate = n_main + 2 * nh
    assert w_in.shape[-1] == col_gate + 2 * d

    x_pair = (x_prompt.reshape(mp, d), x_sample.reshape(ms, d))
    w_in_t = jnp.swapaxes(w_in, 1, 2)
    zeros_pool = jnp.zeros((bp,) + state_pool.shape[2:], F32)
    zeros_conv = jnp.zeros((bp,) + state_conv.shape[2:], F32)
    zeros_rec = jnp.zeros((1, bp) + state_rec.shape[2:], F32)
    pools_p, convs_p, pools_s, convs_s = [], [], [], []
    recs_p = jnp.zeros((depth, bp) + state_rec.shape[2:], F32)
    recs_s = jnp.zeros(state_rec.shape, F32)
    h = ffn_out = None
    for layer in range(depth):
        w_ab = jnp.pad(w_in_t[layer, col_ab:col_gate, :], ((0, LANES - 2 * nh), (0, 0)))
        if layer == 0:
            a = rms_cast_pair(*x_pair, norm_mix[layer])
        else:
            h, a = add_rms_cast(h, ffn_out, norm_mix[layer])
        proj = matmul(a, w_in_t, n_main, layer=layer)
        ab = matmul(a, w_ab, LANES)
        gates = matmul(a, w_in_t[layer, col_gate:, :], 2 * d)
        pw = pool_w[layer].astype(BF16)
        ya_p = pool_mixer(proj, 0, bp, tp, zeros_pool, pw, pool_scale[layer], 0)
        ya_s = pool_mixer(proj, mp, bs, ts, state_pool[layer], pw, pool_scale[layer], PAST_LEN)
        yb_p, recs_p = delta_mixer(proj, ab, 0, bp, tp, dp, zeros_conv, zeros_rec, 0, conv_w[layer], a_log[layer],
                                   dt_bias[layer], o_norm[layer], recs_p, layer)
        yb_s, recs_s = delta_mixer(proj, ab, mp, bs, ts, dp, state_conv[layer], state_rec, layer, conv_w[layer],
                                   a_log[layer], dt_bias[layer], o_norm[layer], recs_s, layer)
        h, f_bf = merge_out_proj((ya_p, ya_s), (yb_p, yb_s), gates, w_branch_a[layer].astype(BF16),
                                 w_branch_b[layer].astype(BF16), w_out[layer].astype(BF16),
                                 x_pair if layer == 0 else h, norm_ffn[layer])
        if layer == 0:
            ffn_out = dense_ffn(f_bf, ffn_w_gate[0], ffn_w_up[0], ffn_w_down[0])
        else:
            y_p, y_s = moe_layer(h, norm_ffn[layer], router[0], moe_w_gate[0], moe_w_up[0], moe_w_down[0],
                                 norm_final, mp)
        pools_p.append(_tail_rows(zeros_pool, proj, 0, tp, 0))
        pools_s.append(_tail_rows(state_pool[layer], proj, mp, ts, 0))
        convs_p.append(_tail_rows(zeros_conv, proj, 0, tp, dp))
        convs_s.append(_tail_rows(state_conv[layer], proj, mp, ts, dp))
    return (y_p.reshape(bp, tp, d), y_s.reshape(bs, ts, d),
            jnp.stack(pools_p), jnp.stack(convs_p), recs_p,
            jnp.stack(pools_s), jnp.stack(convs_s), recs_s)
```

```python
import functools

import jax
import jax.numpy as jnp
import numpy as np
from jax import lax
from jax.experimental import pallas as pl
from jax.experimental.pallas import tpu as pltpu

F32 = jnp.float32
BF16 = jnp.bfloat16
EPS = 1e-6
POOL_WINDOWS = (2, 4, 8, 16)
POOL_HALO = 16
CONV_HALO = 8
CHUNK = 64
TOP_K = 2
PAST_LEN = 16384
LANES = 128
SUB_ROWS = 256
DMA_UNROLL = 8
VMEM_BYTES = 56 * 1024 * 1024


def _params(n_axes, vmem=VMEM_BYTES):
    return pltpu.CompilerParams(dimension_semantics=("arbitrary",) * n_axes, vmem_limit_bytes=vmem)


def _pick(n, pref, mult):
    if n <= pref:
        return n
    for c in range(pref, 0, -1):
        if n % c == 0 and c % mult == 0:
            return c
    raise ValueError(f"no tile for {n} (pref {pref}, mult {mult})")


def _dot(a, b):
    return jnp.dot(a, b, preferred_element_type=F32)


def _dot_nt(a, b):
    return lax.dot_general(a, b, (((1,), (1,)), ((), ())), preferred_element_type=F32)


def _dot_tn(a, b):
    return lax.dot_general(a, b, (((0,), (0,)), ((), ())), preferred_element_type=F32)


def _split3(x):
    hi = x.astype(BF16)
    r = x - hi.astype(F32)
    mid = r.astype(BF16)
    lo = (r - mid.astype(F32)).astype(BF16)
    return hi, mid, lo


def _rms(x, w):
    ms = jnp.mean(x * x, axis=-1, keepdims=True)
    return x * lax.rsqrt(ms + EPS) * w


def _silu(x):
    return x * jax.nn.sigmoid(x)


def _two_source_specs(xa, xb, tm, index_args):
    n_a = xa.shape[0] // tm
    d = xa.shape[1]
    assert xa.shape[0] % tm == 0 and xb.shape[0] % tm == 0
    first = lambda *g: (jnp.minimum(index_args(*g), n_a - 1), 0)
    second = lambda *g: (jnp.maximum(index_args(*g) - n_a, 0), 0)
    return n_a, [pl.BlockSpec((tm, d), first), pl.BlockSpec((tm, d), second)]


def _rms_pair_kernel(xa_ref, xb_ref, w_ref, o_ref, *, n_a):
    x = jnp.where(pl.program_id(0) < n_a, xa_ref[...], xb_ref[...])
    o_ref[...] = _rms(x, w_ref[...]).astype(o_ref.dtype)


def rms_cast_pair(xa, xb, w):
    d = xa.shape[1]
    m = xa.shape[0] + xb.shape[0]
    tm = _pick(int(np.gcd(xa.shape[0], xb.shape[0])), 512, 16)
    n_a, specs = _two_source_specs(xa, xb, tm, lambda i: i)
    return pl.pallas_call(
        functools.partial(_rms_pair_kernel, n_a=n_a),
        grid=(m // tm,),
        in_specs=specs + [pl.BlockSpec((1, d), lambda i: (0, 0))],
        out_specs=pl.BlockSpec((tm, d), lambda i: (i, 0)),
        out_shape=jax.ShapeDtypeStruct((m, d), BF16),
        compiler_params=_params(1),
        name="rms_cast_pair",
    )(xa, xb, w.reshape(1, d))


def _add_rms_kernel(h_ref, y_ref, w_ref, ho_ref, o_ref):
    hn = h_ref[...] + y_ref[...]
    ho_ref[...] = hn
    o_ref[...] = _rms(hn, w_ref[...]).astype(o_ref.dtype)


def add_rms_cast(h, y, w):
    m, d = h.shape
    tm = _pick(m, 512, 16)
    row = pl.BlockSpec((tm, d), lambda i: (i, 0))
    return pl.pallas_call(
        _add_rms_kernel,
        grid=(m // tm,),
        in_specs=[row, row, pl.BlockSpec((1, d), lambda i: (0, 0))],
        out_specs=[row, row],
        out_shape=[jax.ShapeDtypeStruct((m, d), F32), jax.ShapeDtypeStruct((m, d), BF16)],
        compiler_params=_params(1),
        name="add_rms_cast",
    )(h, y, w.reshape(1, d))


def _mm_kernel(x_ref, w_ref, o_ref, wb_ref):
    @pl.when(pl.program_id(1) == 0)
    def _():
        wb_ref[...] = w_ref[...].T.astype(BF16)

    o_ref[...] = _dot(x_ref[...], wb_ref[...])


def matmul(x, w_t, n, layer=None, tm_pref=1024, tn_pref=1024):
    m, k = x.shape
    tm = _pick(m, tm_pref, 16)
    tn = _pick(n, tn_pref, LANES)
    assert w_t.shape[-2] >= n and w_t.shape[-1] == k and w_t.dtype == F32
    if layer is None:
        w_spec = pl.BlockSpec((tn, k), lambda j, i: (j, 0))
    else:
        w_spec = pl.BlockSpec((None, tn, k), lambda j, i: (layer, j, 0))
    return pl.pallas_call(
        _mm_kernel,
        grid=(n // tn, m // tm),
        in_specs=[pl.BlockSpec((tm, k), lambda j, i: (i, 0)), w_spec],
        out_specs=pl.BlockSpec((tm, tn), lambda j, i: (i, j)),
        out_shape=jax.ShapeDtypeStruct((m, n), F32),
        scratch_shapes=[pltpu.VMEM((k, tn), BF16)],
        compiler_params=_params(2),
        name="in_proj",
    )(x, w_t)


def _pool_kernel(u_ref, st_ref, pw_ref, sc_ref, y_ref, ext_ref, *, nb, tt, pos0, gc, rc, n_steps):
    n = pl.program_id(1)
    dp = gc * len(POOL_WINDOWS)

    @pl.when(n == 0)
    def _():
        ext_ref[:, 0:1, :] = jnp.zeros((nb, 1, dp), F32)
        ext_ref[:, 1:POOL_HALO, :] = st_ref[...]

    ext_ref[:, POOL_HALO:POOL_HALO + tt, :] = u_ref[...].reshape(nb, tt, dp)
    for c0 in range(0, tt, rc):
        row = lax.broadcasted_iota(jnp.int32, (1, rc, 1), 1) + (pos0 + c0) + n * tt
        for gi, w in enumerate(POOL_WINDOWS):
            lanes = slice(gi * gc, (gi + 1) * gc)
            base = POOL_HALO + c0
            cur = ext_ref[:, base:base + rc, lanes]
            acc = cur
            for i in range(1, w):
                acc = acc + ext_ref[:, base - i:base - i + rc, lanes]
            cnt = jnp.minimum(row + 1, w).astype(F32)
            d = acc / cnt - cur
            y = _dot(d.reshape(nb * rc, gc).astype(BF16), pw_ref[gi]) * sc_ref[:, lanes]
            if nb == 1:
                y_ref[c0:c0 + rc, lanes] = y
            else:
                y_ref[:, lanes] = y
    if n_steps > 1:
        ext_ref[:, 0:POOL_HALO, :] = ext_ref[:, tt:tt + POOL_HALO, :]


def pool_mixer(proj, row0, b, t, state, pool_w, pool_scale, pos0):
    dp = state.shape[-1]
    gc = dp // len(POOL_WINDOWS)
    if t >= 512:
        nb, tt = 1, 512
    else:
        nb, tt = _pick(b, max(1, 256 // t), 1), t
    assert t % tt == 0 and b % nb == 0 and row0 % (nb * tt) == 0 and tt % 8 == 0
    assert nb == 1 or tt == t
    rc = min(tt, 256) if nb == 1 else tt
    n_steps = t // tt
    rb0 = row0 // (nb * tt)
    kern = functools.partial(_pool_kernel, nb=nb, tt=tt, pos0=pos0, gc=gc, rc=rc, n_steps=n_steps)
    return pl.pallas_call(
        kern,
        grid=(b // nb, n_steps),
        in_specs=[
            pl.BlockSpec((nb * tt, dp), lambda i, n: (rb0 + i * n_steps + n, 0)),
            pl.BlockSpec((nb, POOL_HALO - 1, dp), lambda i, n: (i, 0, 0)),
            pl.BlockSpec((len(POOL_WINDOWS), gc, gc), lambda i, n: (0, 0, 0)),
            pl.BlockSpec((1, dp), lambda i, n: (0, 0)),
        ],
        out_specs=pl.BlockSpec((nb * tt, dp), lambda i, n: (i * n_steps + n, 0)),
        out_shape=jax.ShapeDtypeStruct((b * t, dp), F32),
        scratch_shapes=[pltpu.VMEM((nb, tt + POOL_HALO, dp), F32)],
        compiler_params=_params(2),
        name="pool_mixer",
    )(proj, state, pool_w, pool_scale.reshape(1, dp))


def _delta_kernel(q_ref, k_ref, v_ref, z_ref, ab_ref, cs_ref, rs_ref, cw_ref, al_ref, dtb_ref, on_ref,
                  y_ref, so_ref, ext_ref, s_ref, *, nseq, tc, nh, hk, hv, n_steps, taps):
    n = pl.program_id(1)
    c = nseq * tc
    dq = nh * hk
    halo0 = CONV_HALO - (taps - 1)

    @pl.when(n == 0)
    def _():
        ext_ref[:, halo0:CONV_HALO, :] = cs_ref[...]
        for h in range(nh):
            for b in range(nseq):
                s_ref[h, b * hk:(b + 1) * hk, :] = rs_ref[b, h]

    ext_ref[:, CONV_HALO:CONV_HALO + tc, 0:dq] = q_ref[...].reshape(nseq, tc, dq)
    ext_ref[:, CONV_HALO:CONV_HALO + tc, dq:2 * dq] = k_ref[...].reshape(nseq, tc, dq)
    ext_ref[:, CONV_HALO:CONV_HALO + tc, 2 * dq:2 * dq + nh * hv] = v_ref[...].reshape(nseq, tc, nh * hv)

    def conv(col0, width):
        acc = ext_ref[:, halo0:halo0 + tc, col0:col0 + width] * cw_ref[0:1, col0:col0 + width]
        for j in range(1, taps):
            acc = acc + ext_ref[:, halo0 + j:halo0 + j + tc, col0:col0 + width] * cw_ref[j:j + 1, col0:col0 + width]
        return _silu(acc).reshape(c, width)

    ri = lax.broadcasted_iota(jnp.int32, (c, c), 0)
    ci = lax.broadcasted_iota(jnp.int32, (c, c), 1)
    same = (ri // tc) == (ci // tc)
    low = same & (ci <= ri)
    strict = same & (ci < ri)
    eye = (ri == ci).astype(F32)

    ab = ab_ref[...]
    g_full = -jnp.exp(al_ref[...]) * jax.nn.softplus(ab + dtb_ref[...])
    beta_full = jax.nn.sigmoid(ab)
    parts = _split3(g_full)
    if nseq == 1:
        sel = low.astype(BF16)
        gcum_full = _dot(sel, parts[0]) + _dot(sel, parts[1]) + _dot(sel, parts[2])
        glast_full = jnp.broadcast_to(gcum_full[c - 1:c, :], (c, LANES))
    else:
        sel = jnp.concatenate([low.astype(BF16), same.astype(BF16)], axis=0)
        sums = _dot(sel, parts[0]) + _dot(sel, parts[1]) + _dot(sel, parts[2])
        gcum_full, glast_full = sums[:c], sums[c:]
    pick = (lax.broadcasted_iota(jnp.int32, (8, LANES), 0) == lax.broadcasted_iota(jnp.int32, (8, LANES), 1)).astype(BF16)
    cparts = _split3(gcum_full)
    gcum_t = _dot_nt(pick, cparts[0]) + _dot_nt(pick, cparts[1]) + _dot_nt(pick, cparts[2])

    if nseq > 1:
        rseq = lax.broadcasted_iota(jnp.int32, (c, 1), 0) // tc
        rseq2 = lax.broadcasted_iota(jnp.int32, (2 * c, 1), 0) % c // tc

    def expand(x, seq_of_row):
        if nseq == 1:
            return x
        return jnp.concatenate([jnp.where(seq_of_row == b, x, 0.0) for b in range(nseq)], axis=1)

    levels = int(np.log2(tc))
    assert 2 ** levels == tc
    heads = range(nh)
    s_in = [s_ref[h] for h in heads]
    z_in = [z_ref[:, h * hv:(h + 1) * hv] for h in heads]
    q_c = [conv(h * hk, hk) for h in heads]
    k_c = [conv(dq + h * hk, hk) for h in heads]
    v_c = [conv(2 * dq + h * hv, hv) for h in heads]
    gcs = [gcum_full[:, h:h + 1] for h in heads]
    gls = [glast_full[:, h:h + 1] for h in heads]
    betas = [beta_full[:, nh + h:nh + h + 1] for h in heads]
    decays = [jnp.where(low, jnp.exp(jnp.where(low, gcs[h] - gcum_t[h:h + 1, :], 0.0)), 0.0) for h in heads]
    qs = [q_c[h] * lax.rsqrt(jnp.sum(q_c[h] * q_c[h], axis=-1, keepdims=True) + EPS) * (hk ** -0.5) for h in heads]
    ks = [k_c[h] * lax.rsqrt(jnp.sum(k_c[h] * k_c[h], axis=-1, keepdims=True) + EPS) for h in heads]
    kbs = [ks[h] * betas[h] for h in heads]
    kbf = [ks[h].astype(BF16) for h in heads]
    a_kk = [jnp.where(strict, _dot_nt(kbs[h].astype(BF16), kbf[h]) * decays[h], 0.0) for h in heads]
    a_qk = [(_dot_nt(qs[h].astype(BF16), kbf[h]) * decays[h]).astype(BF16) for h in heads]
    ps = [eye - a_kk[h] for h in heads]
    if levels > 1:
        xb = [(-a_kk[h]).astype(BF16) for h in heads]
        yj = [_dot(xb[h], xb[h]) for h in heads]
        for j in range(1, levels):
            yb = [yj[h].astype(BF16) for h in heads]
            if j < levels - 1:
                r = [_dot(yb[h], jnp.concatenate([ps[h].astype(BF16), yb[h]], axis=1)) for h in heads]
                ps = [ps[h] + r[h][:, :c] for h in heads]
                yj = [r[h][:, c:] for h in heads]
            else:
                ps = [ps[h] + _dot(yb[h], ps[h].astype(BF16)) for h in heads]
    egc = [jnp.exp(gcs[h]) for h in heads]
    uw = [_dot(ps[h].astype(BF16), jnp.concatenate([v_c[h] * betas[h], kbs[h] * egc[h]], axis=1).astype(BF16))
          for h in heads]
    lhs = [expand(jnp.concatenate([uw[h][:, hv:], qs[h] * egc[h]], axis=0), rseq2 if nseq > 1 else None).astype(BF16)
           for h in heads]
    ws = [_dot(lhs[h], s_in[h].astype(BF16)) for h in heads]
    v_new = [(uw[h][:, :hv] - ws[h][:c]).astype(BF16) for h in heads]
    o = [ws[h][c:] + _dot(a_qk[h], v_new[h]) for h in heads]
    kd = [expand(ks[h] * jnp.exp(gls[h] - gcs[h]), rseq if nseq > 1 else None).astype(BF16) for h in heads]
    upd = [_dot_tn(kd[h], v_new[h]) for h in heads]
    if nseq == 1:
        sdec = [jnp.exp(gls[h][0:1, :]) for h in heads]
    else:
        sdec = [jnp.concatenate([jnp.broadcast_to(jnp.exp(gls[h][b * tc:b * tc + 1, :]), (hk, 1))
                                 for b in range(nseq)], axis=0) for h in heads]
    s_out = [s_in[h] * sdec[h] + upd[h] for h in heads]
    y_out = [_rms(o[h], on_ref[...]) * _silu(z_in[h]) for h in heads]
    for h in heads:
        s_ref[h] = s_out[h]
        y_ref[:, h * hv:(h + 1) * hv] = y_out[h]

    if n_steps > 1:
        ext_ref[:, halo0:CONV_HALO, :] = ext_ref[:, halo0 + tc:CONV_HALO + tc, :]

    @pl.when(n == n_steps - 1)
    def _():
        for h in range(nh):
            for b in range(nseq):
                so_ref[b, h] = s_ref[h, b * hk:(b + 1) * hk, :]


def delta_mixer(proj, ab, row0, b, t, col_q, conv_state, rec_state, layer, conv_w, a_log, dt_bias, o_norm,
                rec_out, out_layer):
    _, _, nh, hk, hv = rec_state.shape
    dq, dv = nh * hk, nh * hv
    taps = conv_w.shape[0]
    assert dq == dv and col_q % dq == 0 and nh <= 8 and taps - 1 <= CONV_HALO
    if t >= CHUNK:
        nseq, tc = 1, CHUNK
    else:
        tc = t
        nseq = CHUNK // tc
    assert t % tc == 0 and b % nseq == 0 and tc % 8 == 0
    c = nseq * tc
    n_steps = t // tc
    assert row0 % c == 0
    rb0 = row0 // c
    cq = col_q // dq
    pad = lambda v: jnp.zeros((1, LANES), F32).at[0, :nh].set(v.astype(F32))
    body = functools.partial(_delta_kernel, nseq=nseq, tc=tc, nh=nh, hk=hk, hv=hv, n_steps=n_steps, taps=taps)
    row_map = lambda col: (lambda i, n: (rb0 + i * n_steps + n, col))
    n_in = 11
    n_layers = rec_out.shape[0]
    extra_specs, extra_args, aliases = [pl.BlockSpec(memory_space=pl.ANY)], [rec_out], {n_in: 1}
    kern = lambda *refs: body(*refs[:n_in], *refs[n_in + 1:])
    y, s_out = pl.pallas_call(
        kern,
        grid=(b // nseq, n_steps),
        input_output_aliases=aliases,
        in_specs=[
            pl.BlockSpec((c, dq), row_map(cq)),
            pl.BlockSpec((c, dq), row_map(cq + 1)),
            pl.BlockSpec((c, dv), row_map(cq + 2)),
            pl.BlockSpec((c, dv), row_map(cq + 3)),
            pl.BlockSpec((c, LANES), row_map(0)),
            pl.BlockSpec((nseq, taps - 1, 2 * dq + dv), lambda i, n: (i, 0, 0)),
            pl.BlockSpec((None, nseq, nh, hk, hv), lambda i, n: (layer, i, 0, 0, 0)),
            pl.BlockSpec((taps, 2 * dq + dv), lambda i, n: (0, 0)),
            pl.BlockSpec((1, LANES), lambda i, n: (0, 0)),
            pl.BlockSpec((1, LANES), lambda i, n: (0, 0)),
            pl.BlockSpec((1, hv), lambda i, n: (0, 0)),
        ] + extra_specs,
        out_specs=[
            pl.BlockSpec((c, dv), lambda i, n: (i * n_steps + n, 0)),
            pl.BlockSpec((None, nseq, nh, hk, hv), lambda i, n: (out_layer, i, 0, 0, 0)),
        ],
        out_shape=[jax.ShapeDtypeStruct((b * t, dv), F32), jax.ShapeDtypeStruct((n_layers, b, nh, hk, hv), F32)],
        scratch_shapes=[
            pltpu.VMEM((nseq, tc + CONV_HALO, 2 * dq + dv), F32),
            pltpu.VMEM((nh, nseq * hk, hv), F32),
        ],
        compiler_params=_params(2),
        name="delta_mixer",
    )(proj, proj, proj, proj, ab, conv_state, rec_state, conv_w, pad(a_log), pad(dt_bias), o_norm.reshape(1, hv),
      *extra_args)
    return y, s_out


def _merge_out_kernel(ya0_ref, ya1_ref, yb0_ref, yb1_ref, g_ref, wa_ref, wb_ref, wo_ref, nw_ref, *rest, n_first, two_h):
    if two_h:
        h0_ref, h1_ref, ho_ref, f_ref = rest
    else:
        h_ref, ho_ref, f_ref = rest
    first = pl.program_id(0) < n_first
    d = wo_ref.shape[0]
    y_a = jnp.where(first, ya0_ref[...], ya1_ref[...]).astype(BF16)
    y_b = jnp.where(first, yb0_ref[...], yb1_ref[...]).astype(BF16)
    merged = (jax.nn.sigmoid(g_ref[:, 0:d]) * _dot(y_a, wa_ref[...])
              + jax.nn.sigmoid(g_ref[:, d:2 * d]) * _dot(y_b, wb_ref[...])).astype(BF16)
    h = jnp.where(first, h0_ref[...], h1_ref[...]) if two_h else h_ref[...]
    hn = h + _dot(merged, wo_ref[...])
    ho_ref[...] = hn
    f_ref[...] = _rms(hn, nw_ref[...]).astype(f_ref.dtype)


def merge_out_proj(y_a, y_b, gates, w_a, w_b, w_out, h, norm_w):
    dp, dv = y_a[0].shape[1], y_b[0].shape[1]
    m, d = gates.shape[0], w_out.shape[0]
    tm = _pick(int(np.gcd(y_a[0].shape[0], y_a[1].shape[0])), 256, 16)
    n_first, a_specs = _two_source_specs(y_a[0], y_a[1], tm, lambda i: i)
    _, b_specs = _two_source_specs(y_b[0], y_b[1], tm, lambda i: i)
    two_h = isinstance(h, tuple)
    row = pl.BlockSpec((tm, d), lambda i: (i, 0))
    h_specs = _two_source_specs(h[0], h[1], tm, lambda i: i)[1] if two_h else [row]
    h_args = list(h) if two_h else [h]
    resident = lambda shape: pl.BlockSpec(shape, lambda i: (0, 0), pipeline_mode=pl.Buffered(1))
    return pl.pallas_call(
        functools.partial(_merge_out_kernel, n_first=n_first, two_h=two_h),
        grid=(m // tm,),
        in_specs=a_specs + b_specs + [
            pl.BlockSpec((tm, 2 * d), lambda i: (i, 0)),
            resident((dp, d)), resident((dv, d)), resident((d, d)),
            pl.BlockSpec((1, d), lambda i: (0, 0)),
        ] + h_specs,
        out_specs=[row, row],
        out_shape=[jax.ShapeDtypeStruct((m, d), F32), jax.ShapeDtypeStruct((m, d), BF16)],
        compiler_params=_params(1),
        name="merge_out_proj",
    )(*y_a, *y_b, gates, w_a, w_b, w_out, norm_w.reshape(1, d), *h_args)


def _mlp_kernel(se_ref, ns_ref, x_ref, wg_ref, wu_ref, wd_ref, o_ref, *, n_sub):
    s = pl.program_id(0)
    j = pl.program_id(1)
    n = ns_ref[s]
    n_pairs = n // 2
    odd = n - 2 * n_pairs

    def rows_of(block, n_blocks):
        return pl.ds(pl.multiple_of(block * SUB_ROWS, SUB_ROWS), n_blocks * SUB_ROWS)

    def update(rows, first):
        x = x_ref[rows, :]
        g = _dot(x, wg_ref[...].astype(BF16))
        u = _dot(x, wu_ref[...].astype(BF16))
        y = _dot((_silu(g) * u).astype(BF16), wd_ref[...].astype(BF16))
        if first:
            o_ref[rows, :] = y
        else:
            o_ref[rows, :] += y

    def sweep(first):
        @pl.when(n == n_sub)
        def _():
            update(pl.ds(0, n_sub * SUB_ROWS), first)

        @pl.when(n < n_sub)
        def _():
            def pair(p, carry):
                update(rows_of(2 * p, 2), first)
                return carry

            lax.fori_loop(0, n_pairs, pair, 0)

            @pl.when(odd == 1)
            def _():
                update(rows_of(2 * n_pairs, 1), first)

    @pl.when(j == 0)
    def _():
        sweep(True)

        def blank(r, carry):
            o_ref[rows_of(r, 1), :] = jnp.zeros((SUB_ROWS, o_ref.shape[1]), F32)
            return carry

        lax.fori_loop(n, n_sub, blank, 0)

    @pl.when(j > 0)
    def _():
        sweep(False)


def fused_mlp(x, w_gate, w_up, w_down, tile_expert, tile_sub, tm, tf):
    r, d = x.shape
    e, _, f = w_gate.shape
    n_tiles = r // tm
    n_sub = tm // SUB_ROWS
    nf = f // tf
    assert r % tm == 0 and tm % SUB_ROWS == 0 and f % tf == 0

    def wcol(s, j, se, ns):
        return (se[s], 0, jnp.where(ns[s] > 0, j, nf - 1))

    def wrow(s, j, se, ns):
        return (se[s], jnp.where(ns[s] > 0, j, nf - 1), 0)

    in_specs = [
        pl.BlockSpec((tm, d), lambda s, j, se, ns: (s, 0)),
        pl.BlockSpec((None, d, tf), wcol),
        pl.BlockSpec((None, d, tf), wcol),
        pl.BlockSpec((None, tf, d), wrow),
    ]
    return pl.pallas_call(
        functools.partial(_mlp_kernel, n_sub=n_sub),
        grid_spec=pltpu.PrefetchScalarGridSpec(
            num_scalar_prefetch=2,
            grid=(n_tiles, nf),
            in_specs=in_specs,
            out_specs=pl.BlockSpec((tm, d), lambda s, j, se, ns: (s, 0)),
        ),
        out_shape=jax.ShapeDtypeStruct((r, d), F32),
        compiler_params=_params(2),
        name="fused_mlp",
    )(tile_expert, tile_sub, x, w_gate, w_up, w_down)


def _router_kernel(h_ref, nw_ref, r_ref, f_ref, idx_ref, gate_ref, rank_ref, cnt_ref, seen_ref, *, n_exp):
    @pl.when(pl.program_id(0) == 0)
    def _():
        seen_ref[...] = jnp.zeros(seen_ref.shape, F32)

    f = _rms(h_ref[...], nw_ref[...])
    f_ref[...] = f
    f_hi = f.astype(BF16)
    f_lo = (f - f_hi.astype(F32)).astype(BF16)
    r = r_ref[...]
    r_hi = r.astype(BF16)
    r_lo = (r - r_hi.astype(F32)).astype(BF16)
    logits = _dot(f_hi, r_hi) + (_dot(f_lo, r_hi) + _dot(f_hi, r_lo))
    lane = lax.broadcasted_iota(jnp.int32, logits.shape, 1)
    m1 = jnp.max(logits, axis=-1, keepdims=True)
    i1 = jnp.min(jnp.where(logits == m1, lane, n_exp), axis=-1, keepdims=True)
    rest = jnp.where(lane == i1, -jnp.inf, logits)
    m2 = jnp.max(rest, axis=-1, keepdims=True)
    i2 = jnp.min(jnp.where(rest == m2, lane, n_exp), axis=-1, keepdims=True)
    ex = jnp.exp(m2 - m1)
    den = 1.0 + ex
    two = lax.broadcasted_iota(jnp.int32, idx_ref.shape, 1)
    idx_ref[...] = jnp.where(two == 0, i1, i2)
    gate_ref[...] = jnp.where(two == 0, 1.0 / den, ex / den)
    tm = logits.shape[0]
    chosen = (lane == i1) | (lane == i2)
    earlier = (lax.broadcasted_iota(jnp.int32, (tm, tm), 1) < lax.broadcasted_iota(jnp.int32, (tm, tm), 0))
    before = _dot(earlier.astype(BF16), chosen.astype(BF16)) + seen_ref[...]
    rank1 = jnp.sum(jnp.where(lane == i1, before, 0.0), axis=-1, keepdims=True)
    rank2 = jnp.sum(jnp.where(lane == i2, before, 0.0), axis=-1, keepdims=True)
    rank_ref[...] = jnp.where(two == 0, rank1, rank2).astype(jnp.int32)
    seen = seen_ref[...] + jnp.sum(chosen.astype(F32), axis=0, keepdims=True)
    seen_ref[...] = seen
    cnt_ref[...] = seen.astype(jnp.int32)


def router_top2(h, norm_w, router_w):
    m, d = h.shape
    n_exp = router_w.shape[1]
    tm = _pick(m, 512, 16)
    kern = functools.partial(_router_kernel, n_exp=n_exp)
    return pl.pallas_call(
        kern,
        grid=(m // tm,),
        in_specs=[
            pl.BlockSpec((tm, d), lambda i: (i, 0)),
            pl.BlockSpec((1, d), lambda i: (0, 0)),
            pl.BlockSpec((d, n_exp), lambda i: (0, 0)),
        ],
        out_specs=[
            pl.BlockSpec((tm, d), lambda i: (i, 0)),
            pl.BlockSpec((tm, TOP_K), lambda i: (i, 0)),
            pl.BlockSpec((tm, TOP_K), lambda i: (i, 0)),
            pl.BlockSpec((tm, TOP_K), lambda i: (i, 0)),
            pl.BlockSpec((1, n_exp), lambda i: (0, 0)),
        ],
        out_shape=[
            jax.ShapeDtypeStruct((m, d), F32),
            jax.ShapeDtypeStruct((m, TOP_K), jnp.int32),
            jax.ShapeDtypeStruct((m, TOP_K), F32),
            jax.ShapeDtypeStruct((m, TOP_K), jnp.int32),
            jax.ShapeDtypeStruct((1, n_exp), jnp.int32),
        ],
        scratch_shapes=[pltpu.VMEM((1, n_exp), F32)],
        compiler_params=_params(1),
        name="router_top2",
    )(h, norm_w.reshape(1, d), router_w)


def _gather_kernel(ns_ref, idx_ref, src_ref, dst_ref, buf, sem, *, n_sub):
    n = ns_ref[pl.program_id(0)]
    n_rows = n * SUB_ROWS

    def rows_of(b):
        return pl.ds(pl.multiple_of(b * SUB_ROWS, SUB_ROWS), SUB_ROWS)

    def row_copy(r):
        return pltpu.make_async_copy(src_ref.at[pl.ds(idx_ref[0, r], 1)], buf.at[pl.ds(r, 1)], sem)

    def start(g, carry):
        for u in range(DMA_UNROLL):
            row_copy(g * DMA_UNROLL + u).start(priority=u % 2)
        return carry

    def wait(g, carry):
        for u in range(DMA_UNROLL):
            row_copy(g * DMA_UNROLL + u).wait()
        return carry

    def blank(b, carry):
        dst_ref[rows_of(b), :] = jnp.zeros((SUB_ROWS, dst_ref.shape[1]), dst_ref.dtype)
        return carry

    def cast(b, carry):
        dst_ref[rows_of(b), :] = buf[rows_of(b), :].astype(dst_ref.dtype)
        return carry

    lax.fori_loop(0, n_rows // DMA_UNROLL, start, 0)
    lax.fori_loop(n, n_sub, blank, 0)
    lax.fori_loop(0, n_rows // DMA_UNROLL, wait, 0)
    lax.fori_loop(0, n, cast, 0)


def gather_rows(src, idx, tile_sub, tm, out_dtype=BF16):
    m, d = src.shape
    r = idx.shape[0]
    assert r % tm == 0 and src.dtype == F32
    kern = functools.partial(_gather_kernel, n_sub=tm // SUB_ROWS)
    return pl.pallas_call(
        kern,
        grid_spec=pltpu.PrefetchScalarGridSpec(
            num_scalar_prefetch=1,
            grid=(r // tm,),
            in_specs=[
                pl.BlockSpec((None, 1, tm), lambda s, ns: (s, 0, 0), memory_space=pltpu.SMEM),
                pl.BlockSpec(memory_space=pl.ANY),
            ],
            out_specs=pl.BlockSpec((tm, d), lambda s, ns: (s, 0)),
            scratch_shapes=[pltpu.VMEM((tm, d), F32), pltpu.SemaphoreType.DMA(())],
        ),
        out_shape=jax.ShapeDtypeStruct((r, d), out_dtype),
        compiler_params=_params(1),
        name="gather_rows",
    )(tile_sub, idx.reshape(r // tm, 1, tm), src)


def _combine_kernel(pos_ref, o_ref, g_ref, h_ref, nw_ref, ya_ref, yb_ref, buf, sem, *, tc, n_a):
    def row_copy(r):
        return pltpu.make_async_copy(o_ref.at[pl.ds(pos_ref[r], 1)], buf.at[pl.ds(r, 1)], sem)

    def start(g, carry):
        for u in range(DMA_UNROLL):
            row_copy(g * DMA_UNROLL + u).start(priority=u % 2)
        return carry

    def wait(g, carry):
        for u in range(DMA_UNROLL):
            row_copy(g * DMA_UNROLL + u).wait()
        return carry

    lax.fori_loop(0, TOP_K * tc // DMA_UNROLL, start, 0)
    lax.fori_loop(0, TOP_K * tc // DMA_UNROLL, wait, 0)
    g = g_ref[...]
    acc = h_ref[...]
    for k in range(TOP_K):
        acc = acc + g[:, k:k + 1] * buf[k * tc:(k + 1) * tc, :]
    y = _rms(acc, nw_ref[...])
    i = pl.program_id(0)

    @pl.when(i < n_a)
    def _():
        ya_ref[...] = y

    @pl.when(i >= n_a)
    def _():
        yb_ref[...] = y


def combine_experts(expert_out, pos, gates, h, norm_w, rows_a):
    m, d = h.shape
    tc = _pick(np.gcd(rows_a, m - rows_a), 256, 8)
    n_a = rows_a // tc
    kern = functools.partial(_combine_kernel, tc=tc, n_a=n_a)
    return pl.pallas_call(
        kern,
        grid=(m // tc,),
        in_specs=[
            pl.BlockSpec((TOP_K * tc,), lambda i: (i,), memory_space=pltpu.SMEM),
            pl.BlockSpec(memory_space=pl.ANY),
            pl.BlockSpec((tc, TOP_K), lambda i: (i, 0)),
            pl.BlockSpec((tc, d), lambda i: (i, 0)),
            pl.BlockSpec((1, d), lambda i: (0, 0)),
        ],
        out_specs=[
            pl.BlockSpec((tc, d), lambda i: (jnp.minimum(i, n_a - 1), 0)),
            pl.BlockSpec((tc, d), lambda i: (jnp.maximum(i - n_a, 0), 0)),
        ],
        out_shape=[jax.ShapeDtypeStruct((rows_a, d), F32), jax.ShapeDtypeStruct((m - rows_a, d), F32)],
        scratch_shapes=[pltpu.VMEM((TOP_K * tc, d), F32), pltpu.SemaphoreType.DMA(())],
        compiler_params=_params(1),
        name="combine_experts",
    )(pos.reshape(m // tc, tc, TOP_K).transpose(0, 2, 1).reshape(-1), expert_out, gates, h, norm_w.reshape(1, d))


def routing_tables(top_i, rank, counts, tm):
    m = top_i.shape[0]
    n_exp = counts.shape[0]
    n_pairs = m * TOP_K
    n_tiles = n_pairs // tm + n_exp
    e_flat = top_i.reshape(-1)
    rank = rank.reshape(-1)
    tiles_e = (counts + tm - 1) // tm
    tile_end = jnp.cumsum(tiles_e)
    tile_start = tile_end - tiles_e
    dest = (tile_start * tm)[e_flat] + rank
    src_tok = jnp.zeros((n_tiles * tm,), jnp.int32).at[dest].set(jnp.arange(n_pairs, dtype=jnp.int32) // TOP_K)
    tile = jnp.arange(n_tiles, dtype=jnp.int32)
    used = tile < tile_end[-1]
    t_exp = jnp.minimum(jnp.sum((tile[:, None] >= tile_end[None, :]).astype(jnp.int32), axis=1), n_exp - 1)
    last_exp = t_exp[jnp.maximum(tile_end[-1] - 1, 0)]
    t_exp = jnp.where(used, t_exp, last_exp)
    rows_left = counts[t_exp] - (tile - tile_start[t_exp]) * tm
    t_sub = jnp.where(used, (jnp.clip(rows_left, 0, tm) + SUB_ROWS - 1) // SUB_ROWS, 0)
    return src_tok, dest.reshape(m, TOP_K), t_exp.astype(jnp.int32), t_sub.astype(jnp.int32)


def moe_layer(h, norm_w, router_w, w_gate, w_up, w_down, final_norm_w, rows_a):
    m, d = h.shape
    n_exp = router_w.shape[1]
    f = w_gate.shape[-1]
    tm = 6 * SUB_ROWS if m * TOP_K >= 8192 else SUB_ROWS
    tf = _pick(f, 256, LANES)
    f_norm, top_i, gates, rank, counts = router_top2(h, norm_w, router_w)
    src_tok, pos, t_exp, t_sub = routing_tables(top_i, rank, counts[0], tm)
    x_sorted = gather_rows(f_norm, src_tok, t_sub, tm)
    expert_out = fused_mlp(x_sorted, w_gate, w_up, w_down, t_exp, t_sub, tm, tf)
    return combine_experts(expert_out, pos, gates, h, final_norm_w, rows_a)


def _tail_rows(prev, proj, row0, t, col0):
    b, n, width = prev.shape
    k = min(n, t)
    if k == t or b * t <= 8 * b * k:
        new = lax.slice(proj, (row0, col0), (row0 + b * t, col0 + width)).reshape(b, t, width)[:, t - k:]
    else:
        new = jnp.stack([lax.slice(proj, (row0 + i * t + t - k, col0), (row0 + (i + 1) * t, col0 + width))
                         for i in range(b)])
    return new if k == n else jnp.concatenate([prev[:, t:], new], axis=1)


def dense_ffn(f_bf, w_gate, w_up, w_down):
    m, d = f_bf.shape
    f = w_gate.shape[-1]
    tm = _pick(m, 1024, 2 * SUB_ROWS)
    tf = _pick(f, 256, LANES)
    n_tiles = m // tm
    t_exp = jnp.zeros((n_tiles,), jnp.int32)
    t_sub = jnp.full((n_tiles,), tm // SUB_ROWS, jnp.int32)
    return fused_mlp(f_bf, w_gate[None], w_up[None], w_down[None], t_exp, t_sub, tm, tf)


def kernel(x_prompt, x_sample, state_pool, state_conv, state_rec, norm_mix, w_in, pool_w, pool_scale, conv_w,
           a_log, dt_bias, o_norm, w_branch_a, w_branch_b, w_out, norm_ffn, ffn_w_gate, ffn_w_up, ffn_w_down,
           router, moe_w_gate, moe_w_up, moe_w_down, norm_final):
    bp, tp, d = x_prompt.shape
    bs, ts, _ = x_sample.shape
    depth = w_in.shape[0]
    assert depth == 2, "layer 0 uses the dense FFN, layer 1 the expert FFN followed by the final norm"
    dp = state_pool.shape[-1]
    _, _, nh, hk, hv = state_rec.shape
    dq, dv = nh * hk, nh * hv
    mp, ms = bp * tp, bs * ts
    n_main = dp + 2 * dq + 2 * dv
    col_ab = n_main
    col_gate = n_main + 2 * nh
    assert w_in.shape[-1] == col_gate + 2 * d

    x_pair = (x_prompt.reshape(mp, d), x_sample.reshape(ms, d))
    w_in_t = jnp.swapaxes(w_in, 1, 2)
    zeros_pool = jnp.zeros((bp,) + state_pool.shape[2:], F32)
    zeros_conv = jnp.zeros((bp,) + state_conv.shape[2:], F32)
    zeros_rec = jnp.zeros((1, bp) + state_rec.shape[2:], F32)
    pools_p, convs_p, pools_s, convs_s = [], [], [], []
    recs_p = jnp.zeros((depth, bp) + state_rec.shape[2:], F32)
    recs_s = jnp.zeros(state_rec.shape, F32)
    h = ffn_out = None
    for layer in range(depth):
        w_ab = jnp.pad(w_in_t[layer, col_ab:col_gate, :], ((0, LANES - 2 * nh), (0, 0)))
        if layer == 0:
            a = rms_cast_pair(*x_pair, norm_mix[layer])
        else:
            h, a = add_rms_cast(h, ffn_out, norm_mix[layer])
        proj = matmul(a, w_in_t, n_main, layer=layer)
        ab = matmul(a, w_ab, LANES)
        gates = matmul(a, w_in_t[layer, col_gate:, :], 2 * d)
        pw = pool_w[layer].astype(BF16)
        ya_p = pool_mixer(proj, 0, bp, tp, zeros_pool, pw, pool_scale[layer], 0)
        ya_s = pool_mixer(proj, mp, bs, ts, state_pool[layer], pw, pool_scale[layer], PAST_LEN)
        yb_p, recs_p = delta_mixer(proj, ab, 0, bp, tp, dp, zeros_conv, zeros_rec, 0, conv_w[layer], a_log[layer],
                                   dt_bias[layer], o_norm[layer], recs_p, layer)
        yb_s, recs_s = delta_mixer(proj, ab, mp, bs, ts, dp, state_conv[layer], state_rec, layer, conv_w[layer],
                                   a_log[layer], dt_bias[layer], o_norm[layer], recs_s, layer)
        h, f_bf = merge_out_proj((ya_p, ya_s), (yb_p, yb_s), gates, w_branch_a[layer].astype(BF16),
                                 w_branch_b[layer].astype(BF16), w_out[layer].astype(BF16),
                                 x_pair if layer == 0 else h, norm_ffn[layer])
        if layer == 0:
            ffn_out = dense_ffn(f_bf, ffn_w_gate[0], ffn_w_up[0], ffn_w_down[0])
        else:
            y_p, y_s = moe_layer(h, norm_ffn[layer], router[0], moe_w_gate[0], moe_w_up[0], moe_w_down[0],
                                 norm_final, mp)
        pools_p.append(_tail_rows(zeros_pool, proj, 0, tp, 0))
        pools_s.append(_tail_rows(state_pool[layer], proj, mp, ts, 0))
        convs_p.append(_tail_rows(zeros_conv, proj, 0, tp, dp))
        convs_s.append(_tail_rows(state_conv[layer], proj, mp, ts, dp))
    return (y_p.reshape(bp, tp, d), y_s.reshape(bs, ts, d),
            jnp.stack(pools_p), jnp.stack(convs_p), recs_p,
            jnp.stack(pools_s), jnp.stack(convs_s), recs_s)
```

```python
import functools

import jax
import jax.numpy as jnp
import numpy as np
from jax import lax
from jax.experimental import pallas as pl
from jax.experimental.pallas import tpu as pltpu

F32 = jnp.float32
BF16 = jnp.bfloat16
EPS = 1e-6
POOL_WINDOWS = (2, 4, 8, 16)
POOL_HALO = 16
CONV_HALO = 8
CHUNK = 64
TOP_K = 2
PAST_LEN = 16384
LANES = 128
SUB_ROWS = 256
DMA_UNROLL = 8
VMEM_BYTES = 56 * 1024 * 1024


def _params(n_axes, vmem=VMEM_BYTES):
    return pltpu.CompilerParams(dimension_semantics=("arbitrary",) * n_axes, vmem_limit_bytes=vmem)


def _pick(n, pref, mult):
    if n <= pref:
        return n
    for c in range(pref, 0, -1):
        if n % c == 0 and c % mult == 0:
            return c
    raise ValueError(f"no tile for {n} (pref {pref}, mult {mult})")


def _dot(a, b):
    return jnp.dot(a, b, preferred_element_type=F32)


def _dot_nt(a, b):
    return lax.dot_general(a, b, (((1,), (1,)), ((), ())), preferred_element_type=F32)


def _dot_tn(a, b):
    return lax.dot_general(a, b, (((0,), (0,)), ((), ())), preferred_element_type=F32)


def _split3(x):
    hi = x.astype(BF16)
    r = x - hi.astype(F32)
    mid = r.astype(BF16)
    lo = (r - mid.astype(F32)).astype(BF16)
    return hi, mid, lo


def _rms(x, w):
    ms = jnp.mean(x * x, axis=-1, keepdims=True)
    return x * lax.rsqrt(ms + EPS) * w


def _silu(x):
    return x * jax.nn.sigmoid(x)


def _two_source_specs(xa, xb, tm, index_args):
    n_a = xa.shape[0] // tm
    d = xa.shape[1]
    assert xa.shape[0] % tm == 0 and xb.shape[0] % tm == 0
    first = lambda *g: (jnp.minimum(index_args(*g), n_a - 1), 0)
    second = lambda *g: (jnp.maximum(index_args(*g) - n_a, 0), 0)
    return n_a, [pl.BlockSpec((tm, d), first), pl.BlockSpec((tm, d), second)]


def _rms_pair_kernel(xa_ref, xb_ref, w_ref, o_ref, *, n_a):
    x = jnp.where(pl.program_id(0) < n_a, xa_ref[...], xb_ref[...])
    o_ref[...] = _rms(x, w_ref[...]).astype(o_ref.dtype)


def rms_cast_pair(xa, xb, w):
    d = xa.shape[1]
    m = xa.shape[0] + xb.shape[0]
    tm = _pick(int(np.gcd(xa.shape[0], xb.shape[0])), 512, 16)
    n_a, specs = _two_source_specs(xa, xb, tm, lambda i: i)
    return pl.pallas_call(
        functools.partial(_rms_pair_kernel, n_a=n_a),
        grid=(m // tm,),
        in_specs=specs + [pl.BlockSpec((1, d), lambda i: (0, 0))],
        out_specs=pl.BlockSpec((tm, d), lambda i: (i, 0)),
        out_shape=jax.ShapeDtypeStruct((m, d), BF16),
        compiler_params=_params(1),
        name="rms_cast_pair",
    )(xa, xb, w.reshape(1, d))


def _add_rms_kernel(h_ref, y_ref, w_ref, ho_ref, o_ref):
    hn = h_ref[...] + y_ref[...]
    ho_ref[...] = hn
    o_ref[...] = _rms(hn, w_ref[...]).astype(o_ref.dtype)


def add_rms_cast(h, y, w):
    m, d = h.shape
    tm = _pick(m, 512, 16)
    row = pl.BlockSpec((tm, d), lambda i: (i, 0))
    return pl.pallas_call(
        _add_rms_kernel,
        grid=(m // tm,),
        in_specs=[row, row, pl.BlockSpec((1, d), lambda i: (0, 0))],
        out_specs=[row, row],
        out_shape=[jax.ShapeDtypeStruct((m, d), F32), jax.ShapeDtypeStruct((m, d), BF16)],
        compiler_params=_params(1),
        name="add_rms_cast",
    )(h, y, w.reshape(1, d))


def _mm_kernel(x_ref, w_ref, o_ref, wb_ref):
    @pl.when(pl.program_id(1) == 0)
    def _():
        wb_ref[...] = w_ref[...].T.astype(BF16)

    o_ref[...] = _dot(x_ref[...], wb_ref[...]).astype(o_ref.dtype)


def matmul(x, w_t, n, layer=None, out_dtype=F32, tm_pref=1024, tn_pref=1024):
    m, k = x.shape
    tm = _pick(m, tm_pref, 16)
    tn = _pick(n, tn_pref, LANES)
    assert w_t.shape[-2] >= n and w_t.shape[-1] == k and w_t.dtype == F32
    if layer is None:
        w_spec = pl.BlockSpec((tn, k), lambda j, i: (j, 0))
    else:
        w_spec = pl.BlockSpec((None, tn, k), lambda j, i: (layer, j, 0))
    return pl.pallas_call(
        _mm_kernel,
        grid=(n // tn, m // tm),
        in_specs=[pl.BlockSpec((tm, k), lambda j, i: (i, 0)), w_spec],
        out_specs=pl.BlockSpec((tm, tn), lambda j, i: (i, j)),
        out_shape=jax.ShapeDtypeStruct((m, n), out_dtype),
        scratch_shapes=[pltpu.VMEM((k, tn), BF16)],
        compiler_params=_params(2),
        name="in_proj",
    )(x, w_t)


def _pool_kernel(u_ref, st_ref, pw_ref, sc_ref, y_ref, ext_ref, *, nb, tt, pos0, gc, rc, n_steps):
    n = pl.program_id(1)
    dp = gc * len(POOL_WINDOWS)

    @pl.when(n == 0)
    def _():
        ext_ref[:, 0:1, :] = jnp.zeros((nb, 1, dp), F32)
        ext_ref[:, 1:POOL_HALO, :] = st_ref[...]

    ext_ref[:, POOL_HALO:POOL_HALO + tt, :] = u_ref[...].reshape(nb, tt, dp)
    for c0 in range(0, tt, rc):
        row = lax.broadcasted_iota(jnp.int32, (1, rc, 1), 1) + (pos0 + c0) + n * tt
        for gi, w in enumerate(POOL_WINDOWS):
            lanes = slice(gi * gc, (gi + 1) * gc)
            base = POOL_HALO + c0
            cur = ext_ref[:, base:base + rc, lanes]
            acc = cur
            for i in range(1, w):
                acc = acc + ext_ref[:, base - i:base - i + rc, lanes]
            cnt = jnp.minimum(row + 1, w).astype(F32)
            d = acc / cnt - cur
            y = _dot(d.reshape(nb * rc, gc).astype(BF16), pw_ref[gi]) * sc_ref[:, lanes]
            if nb == 1:
                y_ref[c0:c0 + rc, lanes] = y
            else:
                y_ref[:, lanes] = y
    if n_steps > 1:
        ext_ref[:, 0:POOL_HALO, :] = ext_ref[:, tt:tt + POOL_HALO, :]


def pool_mixer(proj, row0, b, t, state, pool_w, pool_scale, pos0):
    dp = state.shape[-1]
    gc = dp // len(POOL_WINDOWS)
    if t >= 512:
        nb, tt = 1, 512
    else:
        nb, tt = _pick(b, max(1, 256 // t), 1), t
    assert t % tt == 0 and b % nb == 0 and row0 % (nb * tt) == 0 and tt % 8 == 0
    assert nb == 1 or tt == t
    rc = min(tt, 256) if nb == 1 else tt
    n_steps = t // tt
    rb0 = row0 // (nb * tt)
    kern = functools.partial(_pool_kernel, nb=nb, tt=tt, pos0=pos0, gc=gc, rc=rc, n_steps=n_steps)
    return pl.pallas_call(
        kern,
        grid=(b // nb, n_steps),
        in_specs=[
            pl.BlockSpec((nb * tt, dp), lambda i, n: (rb0 + i * n_steps + n, 0)),
            pl.BlockSpec((nb, POOL_HALO - 1, dp), lambda i, n: (i, 0, 0)),
            pl.BlockSpec((len(POOL_WINDOWS), gc, gc), lambda i, n: (0, 0, 0)),
            pl.BlockSpec((1, dp), lambda i, n: (0, 0)),
        ],
        out_specs=pl.BlockSpec((nb * tt, dp), lambda i, n: (i * n_steps + n, 0)),
        out_shape=jax.ShapeDtypeStruct((b * t, dp), F32),
        scratch_shapes=[pltpu.VMEM((nb, tt + POOL_HALO, dp), F32)],
        compiler_params=_params(2),
        name="pool_mixer",
    )(proj, state, pool_w, pool_scale.reshape(1, dp))


def _delta_kernel(q_ref, k_ref, v_ref, z_ref, ab_ref, cs_ref, rs_ref, cw_ref, al_ref, dtb_ref, on_ref,
                  y_ref, so_ref, ext_ref, s_ref, *, nseq, tc, nh, hk, hv, n_steps, taps):
    n = pl.program_id(1)
    c = nseq * tc
    dq = nh * hk
    halo0 = CONV_HALO - (taps - 1)

    @pl.when(n == 0)
    def _():
        ext_ref[:, halo0:CONV_HALO, :] = cs_ref[...]
        for h in range(nh):
            for b in range(nseq):
                s_ref[h, b * hk:(b + 1) * hk, :] = rs_ref[b, h]

    ext_ref[:, CONV_HALO:CONV_HALO + tc, 0:dq] = q_ref[...].reshape(nseq, tc, dq)
    ext_ref[:, CONV_HALO:CONV_HALO + tc, dq:2 * dq] = k_ref[...].reshape(nseq, tc, dq)
    ext_ref[:, CONV_HALO:CONV_HALO + tc, 2 * dq:2 * dq + nh * hv] = v_ref[...].reshape(nseq, tc, nh * hv)

    def conv(col0, width):
        acc = ext_ref[:, halo0:halo0 + tc, col0:col0 + width] * cw_ref[0:1, col0:col0 + width]
        for j in range(1, taps):
            acc = acc + ext_ref[:, halo0 + j:halo0 + j + tc, col0:col0 + width] * cw_ref[j:j + 1, col0:col0 + width]
        return _silu(acc).reshape(c, width)

    ri = lax.broadcasted_iota(jnp.int32, (c, c), 0)
    ci = lax.broadcasted_iota(jnp.int32, (c, c), 1)
    same = (ri // tc) == (ci // tc)
    low = same & (ci <= ri)
    strict = same & (ci < ri)
    eye = (ri == ci).astype(F32)

    ab = ab_ref[...]
    g_full = -jnp.exp(al_ref[...]) * jax.nn.softplus(ab + dtb_ref[...])
    beta_full = jax.nn.sigmoid(ab)
    parts = _split3(g_full)
    if nseq == 1:
        sel = low.astype(BF16)
        gcum_full = _dot(sel, parts[0]) + _dot(sel, parts[1]) + _dot(sel, parts[2])
        glast_full = jnp.broadcast_to(gcum_full[c - 1:c, :], (c, LANES))
    else:
        sel = jnp.concatenate([low.astype(BF16), same.astype(BF16)], axis=0)
        sums = _dot(sel, parts[0]) + _dot(sel, parts[1]) + _dot(sel, parts[2])
        gcum_full, glast_full = sums[:c], sums[c:]
    pick = (lax.broadcasted_iota(jnp.int32, (8, LANES), 0) == lax.broadcasted_iota(jnp.int32, (8, LANES), 1)).astype(BF16)
    cparts = _split3(gcum_full)
    gcum_t = _dot_nt(pick, cparts[0]) + _dot_nt(pick, cparts[1]) + _dot_nt(pick, cparts[2])

    if nseq > 1:
        rseq = lax.broadcasted_iota(jnp.int32, (c, 1), 0) // tc
        rseq2 = lax.broadcasted_iota(jnp.int32, (2 * c, 1), 0) % c // tc

    def expand(x, seq_of_row):
        if nseq == 1:
            return x
        return jnp.concatenate([jnp.where(seq_of_row == b, x, 0.0) for b in range(nseq)], axis=1)

    levels = int(np.log2(tc))
    assert 2 ** levels == tc
    heads = range(nh)
    s_in = [s_ref[h] for h in heads]
    z_in = [z_ref[:, h * hv:(h + 1) * hv] for h in heads]
    q_c = [conv(h * hk, hk) for h in heads]
    k_c = [conv(dq + h * hk, hk) for h in heads]
    v_c = [conv(2 * dq + h * hv, hv) for h in heads]
    gcs = [gcum_full[:, h:h + 1] for h in heads]
    gls = [glast_full[:, h:h + 1] for h in heads]
    betas = [beta_full[:, nh + h:nh + h + 1] for h in heads]
    decays = [jnp.where(low, jnp.exp(jnp.where(low, gcs[h] - gcum_t[h:h + 1, :], 0.0)), 0.0) for h in heads]
    qs = [q_c[h] * lax.rsqrt(jnp.sum(q_c[h] * q_c[h], axis=-1, keepdims=True) + EPS) * (hk ** -0.5) for h in heads]
    ks = [k_c[h] * lax.rsqrt(jnp.sum(k_c[h] * k_c[h], axis=-1, keepdims=True) + EPS) for h in heads]
    kbs = [ks[h] * betas[h] for h in heads]
    kbf = [ks[h].astype(BF16) for h in heads]
    a_kk = [jnp.where(strict, _dot_nt(kbs[h].astype(BF16), kbf[h]) * decays[h], 0.0) for h in heads]
    a_qk = [(_dot_nt(qs[h].astype(BF16), kbf[h]) * decays[h]).astype(BF16) for h in heads]
    ps = [eye - a_kk[h] for h in heads]
    if levels > 1:
        xb = [(-a_kk[h]).astype(BF16) for h in heads]
        yj = [_dot(xb[h], xb[h]) for h in heads]
        for j in range(1, levels):
            yb = [yj[h].astype(BF16) for h in heads]
            if j < levels - 1:
                r = [_dot(yb[h], jnp.concatenate([ps[h].astype(BF16), yb[h]], axis=1)) for h in heads]
                ps = [ps[h] + r[h][:, :c] for h in heads]
                yj = [r[h][:, c:] for h in heads]
            else:
                ps = [ps[h] + _dot(yb[h], ps[h].astype(BF16)) for h in heads]
    egc = [jnp.exp(gcs[h]) for h in heads]
    uw = [_dot(ps[h].astype(BF16), jnp.concatenate([v_c[h] * betas[h], kbs[h] * egc[h]], axis=1).astype(BF16))
          for h in heads]
    lhs = [expand(jnp.concatenate([uw[h][:, hv:], qs[h] * egc[h]], axis=0), rseq2 if nseq > 1 else None).astype(BF16)
           for h in heads]
    ws = [_dot(lhs[h], s_in[h].astype(BF16)) for h in heads]
    v_new = [(uw[h][:, :hv] - ws[h][:c]).astype(BF16) for h in heads]
    o = [ws[h][c:] + _dot(a_qk[h], v_new[h]) for h in heads]
    kd = [expand(ks[h] * jnp.exp(gls[h] - gcs[h]), rseq if nseq > 1 else None).astype(BF16) for h in heads]
    upd = [_dot_tn(kd[h], v_new[h]) for h in heads]
    if nseq == 1:
        sdec = [jnp.exp(gls[h][0:1, :]) for h in heads]
    else:
        sdec = [jnp.concatenate([jnp.broadcast_to(jnp.exp(gls[h][b * tc:b * tc + 1, :]), (hk, 1))
                                 for b in range(nseq)], axis=0) for h in heads]
    s_out = [s_in[h] * sdec[h] + upd[h] for h in heads]
    y_out = [_rms(o[h], on_ref[...]) * _silu(z_in[h]) for h in heads]
    for h in heads:
        s_ref[h] = s_out[h]
        y_ref[:, h * hv:(h + 1) * hv] = y_out[h]

    if n_steps > 1:
        ext_ref[:, halo0:CONV_HALO, :] = ext_ref[:, halo0 + tc:CONV_HALO + tc, :]

    @pl.when(n == n_steps - 1)
    def _():
        for h in range(nh):
            for b in range(nseq):
                so_ref[b, h] = s_ref[h, b * hk:(b + 1) * hk, :]


def delta_mixer(proj, ab, row0, b, t, col_q, conv_state, rec_state, layer, conv_w, a_log, dt_bias, o_norm,
                rec_out, out_layer):
    _, _, nh, hk, hv = rec_state.shape
    dq, dv = nh * hk, nh * hv
    taps = conv_w.shape[0]
    assert dq == dv and col_q % dq == 0 and nh <= 8 and taps - 1 <= CONV_HALO
    if t >= CHUNK:
        nseq, tc = 1, CHUNK
    else:
        tc = t
        nseq = CHUNK // tc
    assert t % tc == 0 and b % nseq == 0 and tc % 8 == 0
    c = nseq * tc
    n_steps = t // tc
    assert row0 % c == 0
    rb0 = row0 // c
    cq = col_q // dq
    pad = lambda v: jnp.zeros((1, LANES), F32).at[0, :nh].set(v.astype(F32))
    body = functools.partial(_delta_kernel, nseq=nseq, tc=tc, nh=nh, hk=hk, hv=hv, n_steps=n_steps, taps=taps)
    row_map = lambda col: (lambda i, n: (rb0 + i * n_steps + n, col))
    n_in = 11
    n_layers = rec_out.shape[0]
    extra_specs, extra_args, aliases = [pl.BlockSpec(memory_space=pl.ANY)], [rec_out], {n_in: 1}
    kern = lambda *refs: body(*refs[:n_in], *refs[n_in + 1:])
    y, s_out = pl.pallas_call(
        kern,
        grid=(b // nseq, n_steps),
        input_output_aliases=aliases,
        in_specs=[
            pl.BlockSpec((c, dq), row_map(cq)),
            pl.BlockSpec((c, dq), row_map(cq + 1)),
            pl.BlockSpec((c, dv), row_map(cq + 2)),
            pl.BlockSpec((c, dv), row_map(cq + 3)),
            pl.BlockSpec((c, LANES), row_map(0)),
            pl.BlockSpec((nseq, taps - 1, 2 * dq + dv), lambda i, n: (i, 0, 0)),
            pl.BlockSpec((None, nseq, nh, hk, hv), lambda i, n: (layer, i, 0, 0, 0)),
            pl.BlockSpec((taps, 2 * dq + dv), lambda i, n: (0, 0)),
            pl.BlockSpec((1, LANES), lambda i, n: (0, 0)),
            pl.BlockSpec((1, LANES), lambda i, n: (0, 0)),
            pl.BlockSpec((1, hv), lambda i, n: (0, 0)),
        ] + extra_specs,
        out_specs=[
            pl.BlockSpec((c, dv), lambda i, n: (i * n_steps + n, 0)),
            pl.BlockSpec((None, nseq, nh, hk, hv), lambda i, n: (out_layer, i, 0, 0, 0)),
        ],
        out_shape=[jax.ShapeDtypeStruct((b * t, dv), F32), jax.ShapeDtypeStruct((n_layers, b, nh, hk, hv), F32)],
        scratch_shapes=[
            pltpu.VMEM((nseq, tc + CONV_HALO, 2 * dq + dv), F32),
            pltpu.VMEM((nh, nseq * hk, hv), F32),
        ],
        compiler_params=_params(2),
        name="delta_mixer",
    )(proj, proj, proj, proj, ab, conv_state, rec_state, conv_w, pad(a_log), pad(dt_bias), o_norm.reshape(1, hv),
      *extra_args)
    return y, s_out


def _merge_out_kernel(ya0_ref, ya1_ref, yb0_ref, yb1_ref, g_ref, wa_ref, wb_ref, wo_ref, nw_ref, *rest, n_first, two_h):
    if two_h:
        h0_ref, h1_ref, ho_ref, f_ref = rest
    else:
        h_ref, ho_ref, f_ref = rest
    first = pl.program_id(0) < n_first
    d = wo_ref.shape[0]
    y_a = jnp.where(first, ya0_ref[...], ya1_ref[...]).astype(BF16)
    y_b = jnp.where(first, yb0_ref[...], yb1_ref[...]).astype(BF16)
    merged = (jax.nn.sigmoid(g_ref[:, 0:d].astype(F32)) * _dot(y_a, wa_ref[...])
              + jax.nn.sigmoid(g_ref[:, d:2 * d].astype(F32)) * _dot(y_b, wb_ref[...])).astype(BF16)
    h = jnp.where(first, h0_ref[...], h1_ref[...]) if two_h else h_ref[...]
    hn = h + _dot(merged, wo_ref[...])
    ho_ref[...] = hn
    f_ref[...] = _rms(hn, nw_ref[...]).astype(f_ref.dtype)


def merge_out_proj(y_a, y_b, gates, w_a, w_b, w_out, h, norm_w):
    dp, dv = y_a[0].shape[1], y_b[0].shape[1]
    m, d = gates.shape[0], w_out.shape[0]
    tm = _pick(int(np.gcd(y_a[0].shape[0], y_a[1].shape[0])), 256, 16)
    n_first, a_specs = _two_source_specs(y_a[0], y_a[1], tm, lambda i: i)
    _, b_specs = _two_source_specs(y_b[0], y_b[1], tm, lambda i: i)
    two_h = isinstance(h, tuple)
    row = pl.BlockSpec((tm, d), lambda i: (i, 0))
    h_specs = _two_source_specs(h[0], h[1], tm, lambda i: i)[1] if two_h else [row]
    h_args = list(h) if two_h else [h]
    resident = lambda shape: pl.BlockSpec(shape, lambda i: (0, 0), pipeline_mode=pl.Buffered(1))
    return pl.pallas_call(
        functools.partial(_merge_out_kernel, n_first=n_first, two_h=two_h),
        grid=(m // tm,),
        in_specs=a_specs + b_specs + [
            pl.BlockSpec((tm, 2 * d), lambda i: (i, 0)),
            resident((dp, d)), resident((dv, d)), resident((d, d)),
            pl.BlockSpec((1, d), lambda i: (0, 0)),
        ] + h_specs,
        out_specs=[row, row],
        out_shape=[jax.ShapeDtypeStruct((m, d), F32), jax.ShapeDtypeStruct((m, d), BF16)],
        compiler_params=_params(1),
        name="merge_out_proj",
    )(*y_a, *y_b, gates, w_a, w_b, w_out, norm_w.reshape(1, d), *h_args)


def _mlp_kernel(se_ref, ns_ref, x_ref, wg_ref, wu_ref, wd_ref, o_ref, *, n_sub):
    s = pl.program_id(0)
    j = pl.program_id(1)
    n = ns_ref[s]
    n_pairs = n // 2
    odd = n - 2 * n_pairs

    def rows_of(block, n_blocks):
        return pl.ds(pl.multiple_of(block * SUB_ROWS, SUB_ROWS), n_blocks * SUB_ROWS)

    def update(rows, first):
        x = x_ref[rows, :]
        g = _dot(x, wg_ref[...].astype(BF16))
        u = _dot(x, wu_ref[...].astype(BF16))
        y = _dot((_silu(g) * u).astype(BF16), wd_ref[...].astype(BF16))
        if first:
            o_ref[rows, :] = y
        else:
            o_ref[rows, :] += y

    def sweep(first):
        @pl.when(n == n_sub)
        def _():
            update(pl.ds(0, n_sub * SUB_ROWS), first)

        @pl.when(n < n_sub)
        def _():
            def pair(p, carry):
                update(rows_of(2 * p, 2), first)
                return carry

            lax.fori_loop(0, n_pairs, pair, 0)

            @pl.when(odd == 1)
            def _():
                update(rows_of(2 * n_pairs, 1), first)

    @pl.when(j == 0)
    def _():
        sweep(True)

        def blank(r, carry):
            o_ref[rows_of(r, 1), :] = jnp.zeros((SUB_ROWS, o_ref.shape[1]), F32)
            return carry

        lax.fori_loop(n, n_sub, blank, 0)

    @pl.when(j > 0)
    def _():
        sweep(False)


def fused_mlp(x, w_gate, w_up, w_down, tile_expert, tile_sub, tm, tf):
    r, d = x.shape
    e, _, f = w_gate.shape
    n_tiles = r // tm
    n_sub = tm // SUB_ROWS
    nf = f // tf
    assert r % tm == 0 and tm % SUB_ROWS == 0 and f % tf == 0

    def wcol(s, j, se, ns):
        return (se[s], 0, jnp.where(ns[s] > 0, j, nf - 1))

    def wrow(s, j, se, ns):
        return (se[s], jnp.where(ns[s] > 0, j, nf - 1), 0)

    in_specs = [
        pl.BlockSpec((tm, d), lambda s, j, se, ns: (s, 0)),
        pl.BlockSpec((None, d, tf), wcol),
        pl.BlockSpec((None, d, tf), wcol),
        pl.BlockSpec((None, tf, d), wrow),
    ]
    return pl.pallas_call(
        functools.partial(_mlp_kernel, n_sub=n_sub),
        grid_spec=pltpu.PrefetchScalarGridSpec(
            num_scalar_prefetch=2,
            grid=(n_tiles, nf),
            in_specs=in_specs,
            out_specs=pl.BlockSpec((tm, d), lambda s, j, se, ns: (s, 0)),
        ),
        out_shape=jax.ShapeDtypeStruct((r, d), F32),
        compiler_params=_params(2),
        name="fused_mlp",
    )(tile_expert, tile_sub, x, w_gate, w_up, w_down)


def _router_kernel(h_ref, nw_ref, r_ref, f_ref, idx_ref, gate_ref, rank_ref, cnt_ref, seen_ref, *, n_exp):
    @pl.when(pl.program_id(0) == 0)
    def _():
        seen_ref[...] = jnp.zeros(seen_ref.shape, F32)

    f = _rms(h_ref[...], nw_ref[...])
    f_ref[...] = f
    f_hi = f.astype(BF16)
    f_lo = (f - f_hi.astype(F32)).astype(BF16)
    r = r_ref[...]
    r_hi = r.astype(BF16)
    r_lo = (r - r_hi.astype(F32)).astype(BF16)
    logits = _dot(f_hi, r_hi) + (_dot(f_lo, r_hi) + _dot(f_hi, r_lo))
    lane = lax.broadcasted_iota(jnp.int32, logits.shape, 1)
    m1 = jnp.max(logits, axis=-1, keepdims=True)
    i1 = jnp.min(jnp.where(logits == m1, lane, n_exp), axis=-1, keepdims=True)
    rest = jnp.where(lane == i1, -jnp.inf, logits)
    m2 = jnp.max(rest, axis=-1, keepdims=True)
    i2 = jnp.min(jnp.where(rest == m2, lane, n_exp), axis=-1, keepdims=True)
    ex = jnp.exp(m2 - m1)
    den = 1.0 + ex
    two = lax.broadcasted_iota(jnp.int32, idx_ref.shape, 1)
    idx_ref[...] = jnp.where(two == 0, i1, i2)
    gate_ref[...] = jnp.where(two == 0, 1.0 / den, ex / den)
    tm = logits.shape[0]
    chosen = (lane == i1) | (lane == i2)
    earlier = (lax.broadcasted_iota(jnp.int32, (tm, tm), 1) < lax.broadcasted_iota(jnp.int32, (tm, tm), 0))
    before = _dot(earlier.astype(BF16), chosen.astype(BF16)) + seen_ref[...]
    rank1 = jnp.sum(jnp.where(lane == i1, before, 0.0), axis=-1, keepdims=True)
    rank2 = jnp.sum(jnp.where(lane == i2, before, 0.0), axis=-1, keepdims=True)
    rank_ref[...] = jnp.where(two == 0, rank1, rank2).astype(jnp.int32)
    seen = seen_ref[...] + jnp.sum(chosen.astype(F32), axis=0, keepdims=True)
    seen_ref[...] = seen
    cnt_ref[...] = seen.astype(jnp.int32)


def router_top2(h, norm_w, router_w):
    m, d = h.shape
    n_exp = router_w.shape[1]
    tm = _pick(m, 512, 16)
    kern = functools.partial(_router_kernel, n_exp=n_exp)
    return pl.pallas_call(
        kern,
        grid=(m // tm,),
        in_specs=[
            pl.BlockSpec((tm, d), lambda i: (i, 0)),
            pl.BlockSpec((1, d), lambda i: (0, 0)),
            pl.BlockSpec((d, n_exp), lambda i: (0, 0)),
        ],
        out_specs=[
            pl.BlockSpec((tm, d), lambda i: (i, 0)),
            pl.BlockSpec((tm, TOP_K), lambda i: (i, 0)),
            pl.BlockSpec((tm, TOP_K), lambda i: (i, 0)),
            pl.BlockSpec((tm, TOP_K), lambda i: (i, 0)),
            pl.BlockSpec((1, n_exp), lambda i: (0, 0)),
        ],
        out_shape=[
            jax.ShapeDtypeStruct((m, d), F32),
            jax.ShapeDtypeStruct((m, TOP_K), jnp.int32),
            jax.ShapeDtypeStruct((m, TOP_K), F32),
            jax.ShapeDtypeStruct((m, TOP_K), jnp.int32),
            jax.ShapeDtypeStruct((1, n_exp), jnp.int32),
        ],
        scratch_shapes=[pltpu.VMEM((1, n_exp), F32)],
        compiler_params=_params(1),
        name="router_top2",
    )(h, norm_w.reshape(1, d), router_w)


def _gather_kernel(ns_ref, idx_ref, src_ref, dst_ref, buf, sem, *, n_sub):
    n = ns_ref[pl.program_id(0)]
    n_rows = n * SUB_ROWS

    def rows_of(b):
        return pl.ds(pl.multiple_of(b * SUB_ROWS, SUB_ROWS), SUB_ROWS)

    def row_copy(r):
        return pltpu.make_async_copy(src_ref.at[pl.ds(idx_ref[0, r], 1)], buf.at[pl.ds(r, 1)], sem)

    def start(g, carry):
        for u in range(DMA_UNROLL):
            row_copy(g * DMA_UNROLL + u).start()
        return carry

    def wait(g, carry):
        for u in range(DMA_UNROLL):
            row_copy(g * DMA_UNROLL + u).wait()
        return carry

    def blank(b, carry):
        dst_ref[rows_of(b), :] = jnp.zeros((SUB_ROWS, dst_ref.shape[1]), dst_ref.dtype)
        return carry

    def cast(b, carry):
        dst_ref[rows_of(b), :] = buf[rows_of(b), :].astype(dst_ref.dtype)
        return carry

    lax.fori_loop(0, n_rows // DMA_UNROLL, start, 0)
    lax.fori_loop(n, n_sub, blank, 0)
    lax.fori_loop(0, n_rows // DMA_UNROLL, wait, 0)
    lax.fori_loop(0, n, cast, 0)


def gather_rows(src, idx, tile_sub, tm, out_dtype=BF16):
    m, d = src.shape
    r = idx.shape[0]
    assert r % tm == 0 and src.dtype == F32
    kern = functools.partial(_gather_kernel, n_sub=tm // SUB_ROWS)
    return pl.pallas_call(
        kern,
        grid_spec=pltpu.PrefetchScalarGridSpec(
            num_scalar_prefetch=1,
            grid=(r // tm,),
            in_specs=[
                pl.BlockSpec((None, 1, tm), lambda s, ns: (s, 0, 0), memory_space=pltpu.SMEM),
                pl.BlockSpec(memory_space=pl.ANY),
            ],
            out_specs=pl.BlockSpec((tm, d), lambda s, ns: (s, 0)),
            scratch_shapes=[pltpu.VMEM((tm, d), F32), pltpu.SemaphoreType.DMA(())],
        ),
        out_shape=jax.ShapeDtypeStruct((r, d), out_dtype),
        compiler_params=_params(1),
        name="gather_rows",
    )(tile_sub, idx.reshape(r // tm, 1, tm), src)


def _combine_kernel(pos_ref, o_ref, g_ref, h_ref, nw_ref, ya_ref, yb_ref, buf, sem, *, tc, n_a):
    def row_copy(r):
        return pltpu.make_async_copy(o_ref.at[pl.ds(pos_ref[r], 1)], buf.at[pl.ds(r, 1)], sem)

    def start(g, carry):
        for u in range(DMA_UNROLL):
            row_copy(g * DMA_UNROLL + u).start()
        return carry

    def wait(g, carry):
        for u in range(DMA_UNROLL):
            row_copy(g * DMA_UNROLL + u).wait()
        return carry

    lax.fori_loop(0, TOP_K * tc // DMA_UNROLL, start, 0)
    lax.fori_loop(0, TOP_K * tc // DMA_UNROLL, wait, 0)
    g = g_ref[...]
    acc = h_ref[...]
    for k in range(TOP_K):
        acc = acc + g[:, k:k + 1] * buf[k * tc:(k + 1) * tc, :]
    y = _rms(acc, nw_ref[...])
    i = pl.program_id(0)

    @pl.when(i < n_a)
    def _():
        ya_ref[...] = y

    @pl.when(i >= n_a)
    def _():
        yb_ref[...] = y


def combine_experts(expert_out, pos, gates, h, norm_w, rows_a):
    m, d = h.shape
    tc = _pick(np.gcd(rows_a, m - rows_a), 256, 8)
    n_a = rows_a // tc
    kern = functools.partial(_combine_kernel, tc=tc, n_a=n_a)
    return pl.pallas_call(
        kern,
        grid=(m // tc,),
        in_specs=[
            pl.BlockSpec((TOP_K * tc,), lambda i: (i,), memory_space=pltpu.SMEM),
            pl.BlockSpec(memory_space=pl.ANY),
            pl.BlockSpec((tc, TOP_K), lambda i: (i, 0)),
            pl.BlockSpec((tc, d), lambda i: (i, 0)),
            pl.BlockSpec((1, d), lambda i: (0, 0)),
        ],
        out_specs=[
            pl.BlockSpec((tc, d), lambda i: (jnp.minimum(i, n_a - 1), 0)),
            pl.BlockSpec((tc, d), lambda i: (jnp.maximum(i - n_a, 0), 0)),
        ],
        out_shape=[jax.ShapeDtypeStruct((rows_a, d), F32), jax.ShapeDtypeStruct((m - rows_a, d), F32)],
        scratch_shapes=[pltpu.VMEM((TOP_K * tc, d), F32), pltpu.SemaphoreType.DMA(())],
        compiler_params=_params(1),
        name="combine_experts",
    )(pos.reshape(m // tc, tc, TOP_K).transpose(0, 2, 1).reshape(-1), expert_out, gates, h, norm_w.reshape(1, d))


def routing_tables(top_i, rank, counts, tm):
    m = top_i.shape[0]
    n_exp = counts.shape[0]
    n_pairs = m * TOP_K
    n_tiles = n_pairs // tm + n_exp
    e_flat = top_i.reshape(-1)
    rank = rank.reshape(-1)
    tiles_e = (counts + tm - 1) // tm
    tile_end = jnp.cumsum(tiles_e)
    tile_start = tile_end - tiles_e
    dest = (tile_start * tm)[e_flat] + rank
    src_tok = jnp.zeros((n_tiles * tm,), jnp.int32).at[dest].set(jnp.arange(n_pairs, dtype=jnp.int32) // TOP_K)
    tile = jnp.arange(n_tiles, dtype=jnp.int32)
    used = tile < tile_end[-1]
    t_exp = jnp.minimum(jnp.sum((tile[:, None] >= tile_end[None, :]).astype(jnp.int32), axis=1), n_exp - 1)
    last_exp = t_exp[jnp.maximum(tile_end[-1] - 1, 0)]
    t_exp = jnp.where(used, t_exp, last_exp)
    rows_left = counts[t_exp] - (tile - tile_start[t_exp]) * tm
    t_sub = jnp.where(used, (jnp.clip(rows_left, 0, tm) + SUB_ROWS - 1) // SUB_ROWS, 0)
    return src_tok, dest.reshape(m, TOP_K), t_exp.astype(jnp.int32), t_sub.astype(jnp.int32)


def moe_layer(h, norm_w, router_w, w_gate, w_up, w_down, final_norm_w, rows_a):
    m, d = h.shape
    n_exp = router_w.shape[1]
    f = w_gate.shape[-1]
    tm = 6 * SUB_ROWS if m * TOP_K >= 8192 else SUB_ROWS
    tf = _pick(f, 256, LANES)
    f_norm, top_i, gates, rank, counts = router_top2(h, norm_w, router_w)
    src_tok, pos, t_exp, t_sub = routing_tables(top_i, rank, counts[0], tm)
    x_sorted = gather_rows(f_norm, src_tok, t_sub, tm)
    expert_out = fused_mlp(x_sorted, w_gate, w_up, w_down, t_exp, t_sub, tm, tf)
    return combine_experts(expert_out, pos, gates, h, final_norm_w, rows_a)


def _tail_rows(prev, proj, row0, t, col0):
    b, n, width = prev.shape
    k = min(n, t)
    if k == t or b * t <= 8 * b * k:
        new = lax.slice(proj, (row0, col0), (row0 + b * t, col0 + width)).reshape(b, t, width)[:, t - k:]
    else:
        new = jnp.stack([lax.slice(proj, (row0 + i * t + t - k, col0), (row0 + (i + 1) * t, col0 + width))
                         for i in range(b)])
    return new if k == n else jnp.concatenate([prev[:, t:], new], axis=1)


def dense_ffn(f_bf, w_gate, w_up, w_down):
    m, d = f_bf.shape
    f = w_gate.shape[-1]
    tm = _pick(m, 1024, 2 * SUB_ROWS)
    tf = _pick(f, 256, LANES)
    n_tiles = m // tm
    t_exp = jnp.zeros((n_tiles,), jnp.int32)
    t_sub = jnp.full((n_tiles,), tm // SUB_ROWS, jnp.int32)
    return fused_mlp(f_bf, w_gate[None], w_up[None], w_down[None], t_exp, t_sub, tm, tf)


def kernel(x_prompt, x_sample, state_pool, state_conv, state_rec, norm_mix, w_in, pool_w, pool_scale, conv_w,
           a_log, dt_bias, o_norm, w_branch_a, w_branch_b, w_out, norm_ffn, ffn_w_gate, ffn_w_up, ffn_w_down,
           router, moe_w_gate, moe_w_up, moe_w_down, norm_final):
    bp, tp, d = x_prompt.shape
    bs, ts, _ = x_sample.shape
    depth = w_in.shape[0]
    assert depth == 2, "layer 0 uses the dense FFN, layer 1 the expert FFN followed by the final norm"
    dp = state_pool.shape[-1]
    _, _, nh, hk, hv = state_rec.shape
    dq, dv = nh * hk, nh * hv
    mp, ms = bp * tp, bs * ts
    n_main = dp + 2 * dq + 2 * dv
    col_ab = n_main
    col_gate = n_main + 2 * nh
    assert w_in.shape[-1] == col_gate + 2 * d

    x_pair = (x_prompt.reshape(mp, d), x_sample.reshape(ms, d))
    w_in_t = jnp.swapaxes(w_in, 1, 2)
    zeros_pool = jnp.zeros((bp,) + state_pool.shape[2:], F32)
    zeros_conv = jnp.zeros((bp,) + state_conv.shape[2:], F32)
    zeros_rec = jnp.zeros((1, bp) + state_rec.shape[2:], F32)
    pools_p, convs_p, pools_s, convs_s = [], [], [], []
    recs_p = jnp.zeros((depth, bp) + state_rec.shape[2:], F32)
    recs_s = jnp.zeros(state_rec.shape, F32)
    h = ffn_out = None
    for layer in range(depth):
        w_ab = jnp.pad(w_in_t[layer, col_ab:col_gate, :], ((0, LANES - 2 * nh), (0, 0)))
        if layer == 0:
            a = rms_cast_pair(*x_pair, norm_mix[layer])
        else:
            h, a = add_rms_cast(h, ffn_out, norm_mix[layer])
        proj = matmul(a, w_in_t, n_main, layer=layer)
        ab = matmul(a, w_ab, LANES)
        gates = matmul(a, w_in_t[layer, col_gate:, :], 2 * d, out_dtype=BF16)
        pw = pool_w[layer].astype(BF16)
        ya_p = pool_mixer(proj, 0, bp, tp, zeros_pool, pw, pool_scale[layer], 0)
        ya_s = pool_mixer(proj, mp, bs, ts, state_pool[layer], pw, pool_scale[layer], PAST_LEN)
        yb_p, recs_p = delta_mixer(proj, ab, 0, bp, tp, dp, zeros_conv, zeros_rec, 0, conv_w[layer], a_log[layer],
                                   dt_bias[layer], o_norm[layer], recs_p, layer)
        yb_s, recs_s = delta_mixer(proj, ab, mp, bs, ts, dp, state_conv[layer], state_rec, layer, conv_w[layer],
                                   a_log[layer], dt_bias[layer], o_norm[layer], recs_s, layer)
        h, f_bf = merge_out_proj((ya_p, ya_s), (yb_p, yb_s), gates, w_branch_a[layer].astype(BF16),
                                 w_branch_b[layer].astype(BF16), w_out[layer].astype(BF16),
                                 x_pair if layer == 0 else h, norm_ffn[layer])
        if layer == 0:
            ffn_out = dense_ffn(f_bf, ffn_w_gate[0], ffn_w_up[0], ffn_w_down[0])
        else:
            y_p, y_s = moe_layer(h, norm_ffn[layer], router[0], moe_w_gate[0], moe_w_up[0], moe_w_down[0],
                                 norm_final, mp)
        pools_p.append(_tail_rows(zeros_pool, proj, 0, tp, 0))
        pools_s.append(_tail_rows(state_pool[layer], proj, mp, ts, 0))
        convs_p.append(_tail_rows(zeros_conv, proj, 0, tp, dp))
        convs_s.append(_tail_rows(state_conv[layer], proj, mp, ts, dp))
    return (y_p.reshape(bp, tp, d), y_s.reshape(bs, ts, d),
            jnp.stack(pools_p), jnp.stack(convs_p), recs_p,
            jnp.stack(pools_s), jnp.stack(convs_s), recs_s)
```

```python
import functools

import jax
import jax.numpy as jnp
import numpy as np
from jax import lax
from jax.experimental import pallas as pl
from jax.experimental.pallas import tpu as pltpu

F32 = jnp.float32
BF16 = jnp.bfloat16
EPS = 1e-6
POOL_WINDOWS = (2, 4, 8, 16)
POOL_HALO = 16
CONV_HALO = 8
CHUNK = 64
TOP_K = 2
PAST_LEN = 16384
LANES = 128
SUB_ROWS = 256
DMA_UNROLL = 8
VMEM_BYTES = 56 * 1024 * 1024


def _params(n_axes, vmem=VMEM_BYTES):
    return pltpu.CompilerParams(dimension_semantics=("arbitrary",) * n_axes, vmem_limit_bytes=vmem)


def _pick(n, pref, mult):
    if n <= pref:
        return n
    for c in range(pref, 0, -1):
        if n % c == 0 and c % mult == 0:
            return c
    raise ValueError(f"no tile for {n} (pref {pref}, mult {mult})")


def _dot(a, b):
    return jnp.dot(a, b, preferred_element_type=F32)


def _dot_nt(a, b):
    return lax.dot_general(a, b, (((1,), (1,)), ((), ())), preferred_element_type=F32)


def _dot_tn(a, b):
    return lax.dot_general(a, b, (((0,), (0,)), ((), ())), preferred_element_type=F32)


def _split3(x):
    hi = x.astype(BF16)
    r = x - hi.astype(F32)
    mid = r.astype(BF16)
    lo = (r - mid.astype(F32)).astype(BF16)
    return hi, mid, lo


def _rms(x, w):
    ms = jnp.mean(x * x, axis=-1, keepdims=True)
    return x * lax.rsqrt(ms + EPS) * w


def _silu(x):
    return x * jax.nn.sigmoid(x)


def _two_source_specs(xa, xb, tm, index_args):
    n_a = xa.shape[0] // tm
    d = xa.shape[1]
    assert xa.shape[0] % tm == 0 and xb.shape[0] % tm == 0
    first = lambda *g: (jnp.minimum(index_args(*g), n_a - 1), 0)
    second = lambda *g: (jnp.maximum(index_args(*g) - n_a, 0), 0)
    return n_a, [pl.BlockSpec((tm, d), first), pl.BlockSpec((tm, d), second)]


def _rms_pair_kernel(xa_ref, xb_ref, w_ref, o_ref, *, n_a):
    x = jnp.where(pl.program_id(0) < n_a, xa_ref[...], xb_ref[...])
    o_ref[...] = _rms(x, w_ref[...]).astype(o_ref.dtype)


def rms_cast_pair(xa, xb, w):
    d = xa.shape[1]
    m = xa.shape[0] + xb.shape[0]
    tm = _pick(int(np.gcd(xa.shape[0], xb.shape[0])), 512, 16)
    n_a, specs = _two_source_specs(xa, xb, tm, lambda i: i)
    return pl.pallas_call(
        functools.partial(_rms_pair_kernel, n_a=n_a),
        grid=(m // tm,),
        in_specs=specs + [pl.BlockSpec((1, d), lambda i: (0, 0))],
        out_specs=pl.BlockSpec((tm, d), lambda i: (i, 0)),
        out_shape=jax.ShapeDtypeStruct((m, d), BF16),
        compiler_params=_params(1),
        name="rms_cast_pair",
    )(xa, xb, w.reshape(1, d))


def _add_rms_kernel(h_ref, y_ref, w_ref, ho_ref, o_ref):
    hn = h_ref[...] + y_ref[...]
    ho_ref[...] = hn
    o_ref[...] = _rms(hn, w_ref[...]).astype(o_ref.dtype)


def add_rms_cast(h, y, w):
    m, d = h.shape
    tm = _pick(m, 512, 16)
    row = pl.BlockSpec((tm, d), lambda i: (i, 0))
    return pl.pallas_call(
        _add_rms_kernel,
        grid=(m // tm,),
        in_specs=[row, row, pl.BlockSpec((1, d), lambda i: (0, 0))],
        out_specs=[row, row],
        out_shape=[jax.ShapeDtypeStruct((m, d), F32), jax.ShapeDtypeStruct((m, d), BF16)],
        compiler_params=_params(1),
        name="add_rms_cast",
    )(h, y, w.reshape(1, d))


def _mm_kernel(x_ref, w_ref, o_ref, wb_ref):
    @pl.when(pl.program_id(1) == 0)
    def _():
        wb_ref[...] = w_ref[...].T.astype(BF16)

    o_ref[...] = _dot(x_ref[...], wb_ref[...])


def matmul(x, w_t, n, layer=None, tm_pref=1024, tn_pref=1024):
    m, k = x.shape
    tm = _pick(m, tm_pref, 16)
    tn = _pick(n, tn_pref, LANES)
    assert w_t.shape[-2] >= n and w_t.shape[-1] == k and w_t.dtype == F32
    if layer is None:
        w_spec = pl.BlockSpec((tn, k), lambda j, i: (j, 0))
    else:
        w_spec = pl.BlockSpec((None, tn, k), lambda j, i: (layer, j, 0))
    return pl.pallas_call(
        _mm_kernel,
        grid=(n // tn, m // tm),
        in_specs=[pl.BlockSpec((tm, k), lambda j, i: (i, 0)), w_spec],
        out_specs=pl.BlockSpec((tm, tn), lambda j, i: (i, j)),
        out_shape=jax.ShapeDtypeStruct((m, n), F32),
        scratch_shapes=[pltpu.VMEM((k, tn), BF16)],
        compiler_params=_params(2),
        name="in_proj",
    )(x, w_t)


def _pool_kernel(u_ref, st_ref, pw_ref, sc_ref, y_ref, ext_ref, *, nb, tt, pos0, gc, rc, n_steps):
    n = pl.program_id(1)
    dp = gc * len(POOL_WINDOWS)

    @pl.when(n == 0)
    def _():
        ext_ref[:, 0:1, :] = jnp.zeros((nb, 1, dp), F32)
        ext_ref[:, 1:POOL_HALO, :] = st_ref[...]

    ext_ref[:, POOL_HALO:POOL_HALO + tt, :] = u_ref[...].reshape(nb, tt, dp)
    for c0 in range(0, tt, rc):
        row = lax.broadcasted_iota(jnp.int32, (1, rc, 1), 1) + (pos0 + c0) + n * tt
        for gi, w in enumerate(POOL_WINDOWS):
            lanes = slice(gi * gc, (gi + 1) * gc)
            base = POOL_HALO + c0
            cur = ext_ref[:, base:base + rc, lanes]
            acc = cur
            for i in range(1, w):
                acc = acc + ext_ref[:, base - i:base - i + rc, lanes]
            cnt = jnp.minimum(row + 1, w).astype(F32)
            d = acc / cnt - cur
            y = _dot(d.reshape(nb * rc, gc).astype(BF16), pw_ref[gi]) * sc_ref[:, lanes]
            if nb == 1:
                y_ref[c0:c0 + rc, lanes] = y
            else:
                y_ref[:, lanes] = y
    if n_steps > 1:
        ext_ref[:, 0:POOL_HALO, :] = ext_ref[:, tt:tt + POOL_HALO, :]


def pool_mixer(proj, row0, b, t, state, pool_w, pool_scale, pos0):
    dp = state.shape[-1]
    gc = dp // len(POOL_WINDOWS)
    if t >= 512:
        nb, tt = 1, 512
    else:
        nb, tt = _pick(b, max(1, 256 // t), 1), t
    assert t % tt == 0 and b % nb == 0 and row0 % (nb * tt) == 0 and tt % 8 == 0
    assert nb == 1 or tt == t
    rc = min(tt, 256) if nb == 1 else tt
    n_steps = t // tt
    rb0 = row0 // (nb * tt)
    kern = functools.partial(_pool_kernel, nb=nb, tt=tt, pos0=pos0, gc=gc, rc=rc, n_steps=n_steps)
    return pl.pallas_call(
        kern,
        grid=(b // nb, n_steps),
        in_specs=[
            pl.BlockSpec((nb * tt, dp), lambda i, n: (rb0 + i * n_steps + n, 0)),
            pl.BlockSpec((nb, POOL_HALO - 1, dp), lambda i, n: (i, 0, 0)),
            pl.BlockSpec((len(POOL_WINDOWS), gc, gc), lambda i, n: (0, 0, 0)),
            pl.BlockSpec((1, dp), lambda i, n: (0, 0)),
        ],
        out_specs=pl.BlockSpec((nb * tt, dp), lambda i, n: (i * n_steps + n, 0)),
        out_shape=jax.ShapeDtypeStruct((b * t, dp), F32),
        scratch_shapes=[pltpu.VMEM((nb, tt + POOL_HALO, dp), F32)],
        compiler_params=_params(2),
        name="pool_mixer",
    )(proj, state, pool_w, pool_scale.reshape(1, dp))


def _delta_kernel(q_ref, k_ref, v_ref, z_ref, ab_ref, cs_ref, rs_ref, cw_ref, al_ref, dtb_ref, on_ref,
                  y_ref, so_ref, ext_ref, s_ref, *, nseq, tc, nh, hk, hv, n_steps, taps):
    n = pl.program_id(1)
    c = nseq * tc
    dq = nh * hk
    halo0 = CONV_HALO - (taps - 1)

    @pl.when(n == 0)
    def _():
        ext_ref[:, halo0:CONV_HALO, :] = cs_ref[...]
        for h in range(nh):
            for b in range(nseq):
                s_ref[h, b * hk:(b + 1) * hk, :] = rs_ref[b, h]

    ext_ref[:, CONV_HALO:CONV_HALO + tc, 0:dq] = q_ref[...].reshape(nseq, tc, dq)
    ext_ref[:, CONV_HALO:CONV_HALO + tc, dq:2 * dq] = k_ref[...].reshape(nseq, tc, dq)
    ext_ref[:, CONV_HALO:CONV_HALO + tc, 2 * dq:2 * dq + nh * hv] = v_ref[...].reshape(nseq, tc, nh * hv)

    def conv(col0, width):
        acc = ext_ref[:, halo0:halo0 + tc, col0:col0 + width] * cw_ref[0:1, col0:col0 + width]
        for j in range(1, taps):
            acc = acc + ext_ref[:, halo0 + j:halo0 + j + tc, col0:col0 + width] * cw_ref[j:j + 1, col0:col0 + width]
        return _silu(acc).reshape(c, width)

    ri = lax.broadcasted_iota(jnp.int32, (c, c), 0)
    ci = lax.broadcasted_iota(jnp.int32, (c, c), 1)
    same = (ri // tc) == (ci // tc)
    low = same & (ci <= ri)
    strict = same & (ci < ri)
    eye = (ri == ci).astype(F32)

    ab = ab_ref[...]
    g_full = -jnp.exp(al_ref[...]) * jax.nn.softplus(ab + dtb_ref[...])
    beta_full = jax.nn.sigmoid(ab)
    parts = _split3(g_full)
    if nseq == 1:
        sel = low.astype(BF16)
        gcum_full = _dot(sel, parts[0]) + _dot(sel, parts[1]) + _dot(sel, parts[2])
        glast_full = jnp.broadcast_to(gcum_full[c - 1:c, :], (c, LANES))
    else:
        sel = jnp.concatenate([low.astype(BF16), same.astype(BF16)], axis=0)
        sums = _dot(sel, parts[0]) + _dot(sel, parts[1]) + _dot(sel, parts[2])
        gcum_full, glast_full = sums[:c], sums[c:]
    pick = (lax.broadcasted_iota(jnp.int32, (8, LANES), 0) == lax.broadcasted_iota(jnp.int32, (8, LANES), 1)).astype(BF16)
    cparts = _split3(gcum_full)
    gcum_t = _dot_nt(pick, cparts[0]) + _dot_nt(pick, cparts[1]) + _dot_nt(pick, cparts[2])

    if nseq > 1:
        rseq = lax.broadcasted_iota(jnp.int32, (c, 1), 0) // tc
        rseq2 = lax.broadcasted_iota(jnp.int32, (2 * c, 1), 0) % c // tc

    def expand(x, seq_of_row):
        if nseq == 1:
            return x
        return jnp.concatenate([jnp.where(seq_of_row == b, x, 0.0) for b in range(nseq)], axis=1)

    levels = int(np.log2(tc))
    assert 2 ** levels == tc
    heads = range(nh)
    s_in = [s_ref[h] for h in heads]
    z_in = [z_ref[:, h * hv:(h + 1) * hv] for h in heads]
    q_c = [conv(h * hk, hk) for h in heads]
    k_c = [conv(dq + h * hk, hk) for h in heads]
    v_c = [conv(2 * dq + h * hv, hv) for h in heads]
    gcs = [gcum_full[:, h:h + 1] for h in heads]
    gls = [glast_full[:, h:h + 1] for h in heads]
    betas = [beta_full[:, nh + h:nh + h + 1] for h in heads]
    decays = [jnp.where(low, jnp.exp(jnp.where(low, gcs[h] - gcum_t[h:h + 1, :], 0.0)), 0.0) for h in heads]
    qs = [q_c[h] * lax.rsqrt(jnp.sum(q_c[h] * q_c[h], axis=-1, keepdims=True) + EPS) * (hk ** -0.5) for h in heads]
    ks = [k_c[h] * lax.rsqrt(jnp.sum(k_c[h] * k_c[h], axis=-1, keepdims=True) + EPS) for h in heads]
    kbs = [ks[h] * betas[h] for h in heads]
    kbf = [ks[h].astype(BF16) for h in heads]
    a_kk = [jnp.where(strict, _dot_nt(kbs[h].astype(BF16), kbf[h]) * decays[h], 0.0) for h in heads]
    a_qk = [(_dot_nt(qs[h].astype(BF16), kbf[h]) * decays[h]).astype(BF16) for h in heads]
    ps = [eye - a_kk[h] for h in heads]
    if levels > 1:
        xb = [(-a_kk[h]).astype(BF16) for h in heads]
        yj = [_dot(xb[h], xb[h]) for h in heads]
        for j in range(1, levels):
            yb = [yj[h].astype(BF16) for h in heads]
            if j < levels - 1:
                r = [_dot(yb[h], jnp.concatenate([ps[h].astype(BF16), yb[h]], axis=1)) for h in heads]
                ps = [ps[h] + r[h][:, :c] for h in heads]
                yj = [r[h][:, c:] for h in heads]
            else:
                ps = [ps[h] + _dot(yb[h], ps[h].astype(BF16)) for h in heads]
    egc = [jnp.exp(gcs[h]) for h in heads]
    uw = [_dot(ps[h].astype(BF16), jnp.concatenate([v_c[h] * betas[h], kbs[h] * egc[h]], axis=1).astype(BF16))
          for h in heads]
    lhs = [expand(jnp.concatenate([uw[h][:, hv:], qs[h] * egc[h]], axis=0), rseq2 if nseq > 1 else None).astype(BF16)
           for h in heads]
    ws = [_dot(lhs[h], s_in[h].astype(BF16)) for h in heads]
    v_new = [(uw[h][:, :hv] - ws[h][:c]).astype(BF16) for h in heads]
    o = [ws[h][c:] + _dot(a_qk[h], v_new[h]) for h in heads]
    kd = [expand(ks[h] * jnp.exp(gls[h] - gcs[h]), rseq if nseq > 1 else None).astype(BF16) for h in heads]
    upd = [_dot_tn(kd[h], v_new[h]) for h in heads]
    if nseq == 1:
        sdec = [jnp.exp(gls[h][0:1, :]) for h in heads]
    else:
        sdec = [jnp.concatenate([jnp.broadcast_to(jnp.exp(gls[h][b * tc:b * tc + 1, :]), (hk, 1))
                                 for b in range(nseq)], axis=0) for h in heads]
    s_out = [s_in[h] * sdec[h] + upd[h] for h in heads]
    y_out = [_rms(o[h], on_ref[...]) * _silu(z_in[h]) for h in heads]
    for h in heads:
        s_ref[h] = s_out[h]
        y_ref[:, h * hv:(h + 1) * hv] = y_out[h]

    if n_steps > 1:
        ext_ref[:, halo0:CONV_HALO, :] = ext_ref[:, halo0 + tc:CONV_HALO + tc, :]

    @pl.when(n == n_steps - 1)
    def _():
        for h in range(nh):
            for b in range(nseq):
                so_ref[b, h] = s_ref[h, b * hk:(b + 1) * hk, :]


def delta_mixer(proj, ab, row0, b, t, col_q, conv_state, rec_state, layer, conv_w, a_log, dt_bias, o_norm,
                rec_out, out_layer):
    _, _, nh, hk, hv = rec_state.shape
    dq, dv = nh * hk, nh * hv
    taps = conv_w.shape[0]
    assert dq == dv and col_q % dq == 0 and nh <= 8 and taps - 1 <= CONV_HALO
    if t >= CHUNK:
        nseq, tc = 1, CHUNK
    else:
        tc = t
        nseq = CHUNK // tc
    assert t % tc == 0 and b % nseq == 0 and tc % 8 == 0
    c = nseq * tc
    n_steps = t // tc
    assert row0 % c == 0
    rb0 = row0 // c
    cq = col_q // dq
    pad = lambda v: jnp.zeros((1, LANES), F32).at[0, :nh].set(v.astype(F32))
    body = functools.partial(_delta_kernel, nseq=nseq, tc=tc, nh=nh, hk=hk, hv=hv, n_steps=n_steps, taps=taps)
    row_map = lambda col: (lambda i, n: (rb0 + i * n_steps + n, col))
    n_in = 11
    n_layers = rec_out.shape[0]
    extra_specs, extra_args, aliases = [pl.BlockSpec(memory_space=pl.ANY)], [rec_out], {n_in: 1}
    kern = lambda *refs: body(*refs[:n_in], *refs[n_in + 1:])
    y, s_out = pl.pallas_call(
        kern,
        grid=(b // nseq, n_steps),
        input_output_aliases=aliases,
        in_specs=[
            pl.BlockSpec((c, dq), row_map(cq)),
            pl.BlockSpec((c, dq), row_map(cq + 1)),
            pl.BlockSpec((c, dv), row_map(cq + 2)),
            pl.BlockSpec((c, dv), row_map(cq + 3)),
            pl.BlockSpec((c, LANES), row_map(0)),
            pl.BlockSpec((nseq, taps - 1, 2 * dq + dv), lambda i, n: (i, 0, 0)),
            pl.BlockSpec((None, nseq, nh, hk, hv), lambda i, n: (layer, i, 0, 0, 0)),
            pl.BlockSpec((taps, 2 * dq + dv), lambda i, n: (0, 0)),
            pl.BlockSpec((1, LANES), lambda i, n: (0, 0)),
            pl.BlockSpec((1, LANES), lambda i, n: (0, 0)),
            pl.BlockSpec((1, hv), lambda i, n: (0, 0)),
        ] + extra_specs,
        out_specs=[
            pl.BlockSpec((c, dv), lambda i, n: (i * n_steps + n, 0)),
            pl.BlockSpec((None, nseq, nh, hk, hv), lambda i, n: (out_layer, i, 0, 0, 0)),
        ],
        out_shape=[jax.ShapeDtypeStruct((b * t, dv), F32), jax.ShapeDtypeStruct((n_layers, b, nh, hk, hv), F32)],
        scratch_shapes=[
            pltpu.VMEM((nseq, tc + CONV_HALO, 2 * dq + dv), F32),
            pltpu.VMEM((nh, nseq * hk, hv), F32),
        ],
        compiler_params=_params(2),
        name="delta_mixer",
    )(proj, proj, proj, proj, ab, conv_state, rec_state, conv_w, pad(a_log), pad(dt_bias), o_norm.reshape(1, hv),
      *extra_args)
    return y, s_out


def _merge_out_kernel(ya0_ref, ya1_ref, yb0_ref, yb1_ref, g_ref, wa_ref, wb_ref, wo_ref, nw_ref, *rest, n_first, two_h):
    if two_h:
        h0_ref, h1_ref, ho_ref, f_ref = rest
    else:
        h_ref, ho_ref, f_ref = rest
    first = pl.program_id(0) < n_first
    d = wo_ref.shape[0]
    y_a = jnp.where(first, ya0_ref[...], ya1_ref[...]).astype(BF16)
    y_b = jnp.where(first, yb0_ref[...], yb1_ref[...]).astype(BF16)
    merged = (jax.nn.sigmoid(g_ref[:, 0:d]) * _dot(y_a, wa_ref[...])
              + jax.nn.sigmoid(g_ref[:, d:2 * d]) * _dot(y_b, wb_ref[...])).astype(BF16)
    h = jnp.where(first, h0_ref[...], h1_ref[...]) if two_h else h_ref[...]
    hn = h + _dot(merged, wo_ref[...])
    ho_ref[...] = hn
    f_ref[...] = _rms(hn, nw_ref[...]).astype(f_ref.dtype)


def merge_out_proj(y_a, y_b, gates, w_a, w_b, w_out, h, norm_w):
    dp, dv = y_a[0].shape[1], y_b[0].shape[1]
    m, d = gates.shape[0], w_out.shape[0]
    tm = _pick(int(np.gcd(y_a[0].shape[0], y_a[1].shape[0])), 256, 16)
    n_first, a_specs = _two_source_specs(y_a[0], y_a[1], tm, lambda i: i)
    _, b_specs = _two_source_specs(y_b[0], y_b[1], tm, lambda i: i)
    two_h = isinstance(h, tuple)
    row = pl.BlockSpec((tm, d), lambda i: (i, 0))
    h_specs = _two_source_specs(h[0], h[1], tm, lambda i: i)[1] if two_h else [row]
    h_args = list(h) if two_h else [h]
    resident = lambda shape: pl.BlockSpec(shape, lambda i: (0, 0), pipeline_mode=pl.Buffered(1))
    return pl.pallas_call(
        functools.partial(_merge_out_kernel, n_first=n_first, two_h=two_h),
        grid=(m // tm,),
        in_specs=a_specs + b_specs + [
            pl.BlockSpec((tm, 2 * d), lambda i: (i, 0)),
            resident((dp, d)), resident((dv, d)), resident((d, d)),
            pl.BlockSpec((1, d), lambda i: (0, 0)),
        ] + h_specs,
        out_specs=[row, row],
        out_shape=[jax.ShapeDtypeStruct((m, d), F32), jax.ShapeDtypeStruct((m, d), BF16)],
        compiler_params=_params(1),
        name="merge_out_proj",
    )(*y_a, *y_b, gates, w_a, w_b, w_out, norm_w.reshape(1, d), *h_args)


def _mlp_kernel(se_ref, ns_ref, x_ref, wg_ref, wu_ref, wd_ref, o_ref, *, n_sub):
    s = pl.program_id(0)
    j = pl.program_id(1)
    n = ns_ref[s]
    n_pairs = n // 2
    odd = n - 2 * n_pairs

    def rows_of(block, n_blocks):
        return pl.ds(pl.multiple_of(block * SUB_ROWS, SUB_ROWS), n_blocks * SUB_ROWS)

    def update(rows, first):
        x = x_ref[rows, :]
        g = _dot(x, wg_ref[...].astype(BF16))
        u = _dot(x, wu_ref[...].astype(BF16))
        y = _dot((_silu(g) * u).astype(BF16), wd_ref[...].astype(BF16))
        if first:
            o_ref[rows, :] = y
        else:
            o_ref[rows, :] += y

    def sweep(first):
        @pl.when(n == n_sub)
        def _():
            update(pl.ds(0, n_sub * SUB_ROWS), first)

        @pl.when(n == 3)
        def _():
            update(pl.ds(0, 3 * SUB_ROWS), first)

        @pl.when((n < n_sub) & (n != 3))
        def _():
            def pair(p, carry):
                update(rows_of(2 * p, 2), first)
                return carry

            lax.fori_loop(0, n_pairs, pair, 0)

            @pl.when(odd == 1)
            def _():
                update(rows_of(2 * n_pairs, 1), first)

    @pl.when(j == 0)
    def _():
        sweep(True)

        def blank(r, carry):
            o_ref[rows_of(r, 1), :] = jnp.zeros((SUB_ROWS, o_ref.shape[1]), F32)
            return carry

        lax.fori_loop(n, n_sub, blank, 0)

    @pl.when(j > 0)
    def _():
        sweep(False)


def fused_mlp(x, w_gate, w_up, w_down, tile_expert, tile_sub, tm, tf):
    r, d = x.shape
    e, _, f = w_gate.shape
    n_tiles = r // tm
    n_sub = tm // SUB_ROWS
    nf = f // tf
    assert r % tm == 0 and tm % SUB_ROWS == 0 and f % tf == 0

    def wcol(s, j, se, ns):
        return (se[s], 0, jnp.where(ns[s] > 0, j, nf - 1))

    def wrow(s, j, se, ns):
        return (se[s], jnp.where(ns[s] > 0, j, nf - 1), 0)

    in_specs = [
        pl.BlockSpec((tm, d), lambda s, j, se, ns: (s, 0)),
        pl.BlockSpec((None, d, tf), wcol),
        pl.BlockSpec((None, d, tf), wcol),
        pl.BlockSpec((None, tf, d), wrow),
    ]
    return pl.pallas_call(
        functools.partial(_mlp_kernel, n_sub=n_sub),
        grid_spec=pltpu.PrefetchScalarGridSpec(
            num_scalar_prefetch=2,
            grid=(n_tiles, nf),
            in_specs=in_specs,
            out_specs=pl.BlockSpec((tm, d), lambda s, j, se, ns: (s, 0)),
        ),
        out_shape=jax.ShapeDtypeStruct((r, d), F32),
        compiler_params=_params(2),
        name="fused_mlp",
    )(tile_expert, tile_sub, x, w_gate, w_up, w_down)


def _router_kernel(h_ref, nw_ref, r_ref, f_ref, idx_ref, gate_ref, rank_ref, cnt_ref, seen_ref, *, n_exp):
    @pl.when(pl.program_id(0) == 0)
    def _():
        seen_ref[...] = jnp.zeros(seen_ref.shape, F32)

    f = _rms(h_ref[...], nw_ref[...])
    f_ref[...] = f
    f_hi = f.astype(BF16)
    f_lo = (f - f_hi.astype(F32)).astype(BF16)
    r = r_ref[...]
    r_hi = r.astype(BF16)
    r_lo = (r - r_hi.astype(F32)).astype(BF16)
    logits = _dot(f_hi, r_hi) + (_dot(f_lo, r_hi) + _dot(f_hi, r_lo))
    lane = lax.broadcasted_iota(jnp.int32, logits.shape, 1)
    m1 = jnp.max(logits, axis=-1, keepdims=True)
    i1 = jnp.min(jnp.where(logits == m1, lane, n_exp), axis=-1, keepdims=True)
    rest = jnp.where(lane == i1, -jnp.inf, logits)
    m2 = jnp.max(rest, axis=-1, keepdims=True)
    i2 = jnp.min(jnp.where(rest == m2, lane, n_exp), axis=-1, keepdims=True)
    ex = jnp.exp(m2 - m1)
    den = 1.0 + ex
    two = lax.broadcasted_iota(jnp.int32, idx_ref.shape, 1)
    idx_ref[...] = jnp.where(two == 0, i1, i2)
    gate_ref[...] = jnp.where(two == 0, 1.0 / den, ex / den)
    tm = logits.shape[0]
    chosen = (lane == i1) | (lane == i2)
    earlier = (lax.broadcasted_iota(jnp.int32, (tm, tm), 1) < lax.broadcasted_iota(jnp.int32, (tm, tm), 0))
    before = _dot(earlier.astype(BF16), chosen.astype(BF16)) + seen_ref[...]
    rank1 = jnp.sum(jnp.where(lane == i1, before, 0.0), axis=-1, keepdims=True)
    rank2 = jnp.sum(jnp.where(lane == i2, before, 0.0), axis=-1, keepdims=True)
    rank_ref[...] = jnp.where(two == 0, rank1, rank2).astype(jnp.int32)
    seen = seen_ref[...] + jnp.sum(chosen.astype(F32), axis=0, keepdims=True)
    seen_ref[...] = seen
    cnt_ref[...] = seen.astype(jnp.int32)


def router_top2(h, norm_w, router_w):
    m, d = h.shape
    n_exp = router_w.shape[1]
    tm = _pick(m, 512, 16)
    kern = functools.partial(_router_kernel, n_exp=n_exp)
    return pl.pallas_call(
        kern,
        grid=(m // tm,),
        in_specs=[
            pl.BlockSpec((tm, d), lambda i: (i, 0)),
            pl.BlockSpec((1, d), lambda i: (0, 0)),
            pl.BlockSpec((d, n_exp), lambda i: (0, 0)),
        ],
        out_specs=[
            pl.BlockSpec((tm, d), lambda i: (i, 0)),
            pl.BlockSpec((tm, TOP_K), lambda i: (i, 0)),
            pl.BlockSpec((tm, TOP_K), lambda i: (i, 0)),
            pl.BlockSpec((tm, TOP_K), lambda i: (i, 0)),
            pl.BlockSpec((1, n_exp), lambda i: (0, 0)),
        ],
        out_shape=[
            jax.ShapeDtypeStruct((m, d), F32),
            jax.ShapeDtypeStruct((m, TOP_K), jnp.int32),
            jax.ShapeDtypeStruct((m, TOP_K), F32),
            jax.ShapeDtypeStruct((m, TOP_K), jnp.int32),
            jax.ShapeDtypeStruct((1, n_exp), jnp.int32),
        ],
        scratch_shapes=[pltpu.VMEM((1, n_exp), F32)],
        compiler_params=_params(1),
        name="router_top2",
    )(h, norm_w.reshape(1, d), router_w)


def _gather_kernel(ns_ref, idx_ref, src_ref, dst_ref, buf, sem, *, n_sub):
    n = ns_ref[pl.program_id(0)]
    n_rows = n * SUB_ROWS

    def rows_of(b):
        return pl.ds(pl.multiple_of(b * SUB_ROWS, SUB_ROWS), SUB_ROWS)

    def row_copy(r):
        return pltpu.make_async_copy(src_ref.at[pl.ds(idx_ref[0, r], 1)], buf.at[pl.ds(r, 1)], sem)

    def start(g, carry):
        for u in range(DMA_UNROLL):
            row_copy(g * DMA_UNROLL + u).start()
        return carry

    def wait(g, carry):
        for u in range(DMA_UNROLL):
            row_copy(g * DMA_UNROLL + u).wait()
        return carry

    def blank(b, carry):
        dst_ref[rows_of(b), :] = jnp.zeros((SUB_ROWS, dst_ref.shape[1]), dst_ref.dtype)
        return carry

    def cast(b, carry):
        dst_ref[rows_of(b), :] = buf[rows_of(b), :].astype(dst_ref.dtype)
        return carry

    lax.fori_loop(0, n_rows // DMA_UNROLL, start, 0)
    lax.fori_loop(n, n_sub, blank, 0)
    lax.fori_loop(0, n_rows // DMA_UNROLL, wait, 0)
    lax.fori_loop(0, n, cast, 0)


def gather_rows(src, idx, tile_sub, tm, out_dtype=BF16):
    m, d = src.shape
    r = idx.shape[0]
    assert r % tm == 0 and src.dtype == F32
    kern = functools.partial(_gather_kernel, n_sub=tm // SUB_ROWS)
    return pl.pallas_call(
        kern,
        grid_spec=pltpu.PrefetchScalarGridSpec(
            num_scalar_prefetch=1,
            grid=(r // tm,),
            in_specs=[
                pl.BlockSpec((None, 1, tm), lambda s, ns: (s, 0, 0), memory_space=pltpu.SMEM),
                pl.BlockSpec(memory_space=pl.ANY),
            ],
            out_specs=pl.BlockSpec((tm, d), lambda s, ns: (s, 0)),
            scratch_shapes=[pltpu.VMEM((tm, d), F32), pltpu.SemaphoreType.DMA(())],
        ),
        out_shape=jax.ShapeDtypeStruct((r, d), out_dtype),
        compiler_params=_params(1),
        name="gather_rows",
    )(tile_sub, idx.reshape(r // tm, 1, tm), src)


def _combine_kernel(pos_ref, o_ref, g_ref, h_ref, nw_ref, ya_ref, yb_ref, buf, sem, *, tc, n_a):
    def row_copy(r):
        return pltpu.make_async_copy(o_ref.at[pl.ds(pos_ref[r], 1)], buf.at[pl.ds(r, 1)], sem)

    def start(g, carry):
        for u in range(DMA_UNROLL):
            row_copy(g * DMA_UNROLL + u).start()
        return carry

    def wait(g, carry):
        for u in range(DMA_UNROLL):
            row_copy(g * DMA_UNROLL + u).wait()
        return carry

    lax.fori_loop(0, TOP_K * tc // DMA_UNROLL, start, 0)
    lax.fori_loop(0, TOP_K * tc // DMA_UNROLL, wait, 0)
    g = g_ref[...]
    acc = h_ref[...]
    for k in range(TOP_K):
        acc = acc + g[:, k:k + 1] * buf[k * tc:(k + 1) * tc, :]
    y = _rms(acc, nw_ref[...])
    i = pl.program_id(0)

    @pl.when(i < n_a)
    def _():
        ya_ref[...] = y

    @pl.when(i >= n_a)
    def _():
        yb_ref[...] = y


def combine_experts(expert_out, pos, gates, h, norm_w, rows_a):
    m, d = h.shape
    tc = _pick(np.gcd(rows_a, m - rows_a), 256, 8)
    n_a = rows_a // tc
    kern = functools.partial(_combine_kernel, tc=tc, n_a=n_a)
    return pl.pallas_call(
        kern,
        grid=(m // tc,),
        in_specs=[
            pl.BlockSpec((TOP_K * tc,), lambda i: (i,), memory_space=pltpu.SMEM),
            pl.BlockSpec(memory_space=pl.ANY),
            pl.BlockSpec((tc, TOP_K), lambda i: (i, 0)),
            pl.BlockSpec((tc, d), lambda i: (i, 0)),
            pl.BlockSpec((1, d), lambda i: (0, 0)),
        ],
        out_specs=[
            pl.BlockSpec((tc, d), lambda i: (jnp.minimum(i, n_a - 1), 0)),
            pl.BlockSpec((tc, d), lambda i: (jnp.maximum(i - n_a, 0), 0)),
        ],
        out_shape=[jax.ShapeDtypeStruct((rows_a, d), F32), jax.ShapeDtypeStruct((m - rows_a, d), F32)],
        scratch_shapes=[pltpu.VMEM((TOP_K * tc, d), F32), pltpu.SemaphoreType.DMA(())],
        compiler_params=_params(1),
        name="combine_experts",
    )(pos.reshape(m // tc, tc, TOP_K).transpose(0, 2, 1).reshape(-1), expert_out, gates, h, norm_w.reshape(1, d))


def routing_tables(top_i, rank, counts, tm):
    m = top_i.shape[0]
    n_exp = counts.shape[0]
    n_pairs = m * TOP_K
    n_tiles = n_pairs // tm + n_exp
    e_flat = top_i.reshape(-1)
    rank = rank.reshape(-1)
    tiles_e = (counts + tm - 1) // tm
    tile_end = jnp.cumsum(tiles_e)
    tile_start = tile_end - tiles_e
    dest = (tile_start * tm)[e_flat] + rank
    src_tok = jnp.zeros((n_tiles * tm,), jnp.int32).at[dest].set(jnp.arange(n_pairs, dtype=jnp.int32) // TOP_K)
    tile = jnp.arange(n_tiles, dtype=jnp.int32)
    used = tile < tile_end[-1]
    t_exp = jnp.minimum(jnp.sum((tile[:, None] >= tile_end[None, :]).astype(jnp.int32), axis=1), n_exp - 1)
    last_exp = t_exp[jnp.maximum(tile_end[-1] - 1, 0)]
    t_exp = jnp.where(used, t_exp, last_exp)
    rows_left = counts[t_exp] - (tile - tile_start[t_exp]) * tm
    t_sub = jnp.where(used, (jnp.clip(rows_left, 0, tm) + SUB_ROWS - 1) // SUB_ROWS, 0)
    return src_tok, dest.reshape(m, TOP_K), t_exp.astype(jnp.int32), t_sub.astype(jnp.int32)


def moe_layer(h, norm_w, router_w, w_gate, w_up, w_down, final_norm_w, rows_a):
    m, d = h.shape
    n_exp = router_w.shape[1]
    f = w_gate.shape[-1]
    tm = 6 * SUB_ROWS if m * TOP_K >= 8192 else SUB_ROWS
    tf = _pick(f, 256, LANES)
    f_norm, top_i, gates, rank, counts = router_top2(h, norm_w, router_w)
    src_tok, pos, t_exp, t_sub = routing_tables(top_i, rank, counts[0], tm)
    x_sorted = gather_rows(f_norm, src_tok, t_sub, tm)
    expert_out = fused_mlp(x_sorted, w_gate, w_up, w_down, t_exp, t_sub, tm, tf)
    return combine_experts(expert_out, pos, gates, h, final_norm_w, rows_a)


def _tail_rows(prev, proj, row0, t, col0):
    b, n, width = prev.shape
    k = min(n, t)
    if k == t or b * t <= 8 * b * k:
        new = lax.slice(proj, (row0, col0), (row0 + b * t, col0 + width)).reshape(b, t, width)[:, t - k:]
    else:
        new = jnp.stack([lax.slice(proj, (row0 + i * t + t - k, col0), (row0 + (i + 1) * t, col0 + width))
                         for i in range(b)])
    return new if k == n else jnp.concatenate([prev[:, t:], new], axis=1)


def dense_ffn(f_bf, w_gate, w_up, w_down):
    m, d = f_bf.shape
    f = w_gate.shape[-1]
    tm = _pick(m, 1024, 2 * SUB_ROWS)
    tf = _pick(f, 256, LANES)
    n_tiles = m // tm
    t_exp = jnp.zeros((n_tiles,), jnp.int32)
    t_sub = jnp.full((n_tiles,), tm // SUB_ROWS, jnp.int32)
    return fused_mlp(f_bf, w_gate[None], w_up[None], w_down[None], t_exp, t_sub, tm, tf)


def kernel(x_prompt, x_sample, state_pool, state_conv, state_rec, norm_mix, w_in, pool_w, pool_scale, conv_w,
           a_log, dt_bias, o_norm, w_branch_a, w_branch_b, w_out, norm_ffn, ffn_w_gate, ffn_w_up, ffn_w_down,
           router, moe_w_gate, moe_w_up, moe_w_down, norm_final):
    bp, tp, d = x_prompt.shape
    bs, ts, _ = x_sample.shape
    depth = w_in.shape[0]
    assert depth == 2, "layer 0 uses the dense FFN, layer 1 the expert FFN followed by the final norm"
    dp = state_pool.shape[-1]
    _, _, nh, hk, hv = state_rec.shape
    dq, dv = nh * hk, nh * hv
    mp, ms = bp * tp, bs * ts
    n_main = dp + 2 * dq + 2 * dv
    col_ab = n_main
    col_gate = n_main + 2 * nh
    assert w_in.shape[-1] == col_gate + 2 * d

    x_pair = (x_prompt.reshape(mp, d), x_sample.reshape(ms, d))
    w_in_t = jnp.swapaxes(w_in, 1, 2)
    zeros_pool = jnp.zeros((bp,) + state_pool.shape[2:], F32)
    zeros_conv = jnp.zeros((bp,) + state_conv.shape[2:], F32)
    zeros_rec = jnp.zeros((1, bp) + state_rec.shape[2:], F32)
    pools_p, convs_p, pools_s, convs_s = [], [], [], []
    recs_p = jnp.zeros((depth, bp) + state_rec.shape[2:], F32)
    recs_s = jnp.zeros(state_rec.shape, F32)
    h = ffn_out = None
    for layer in range(depth):
        w_ab = jnp.pad(w_in_t[layer, col_ab:col_gate, :], ((0, LANES - 2 * nh), (0, 0)))
        if layer == 0:
            a = rms_cast_pair(*x_pair, norm_mix[layer])
        else:
            h, a = add_rms_cast(h, ffn_out, norm_mix[layer])
        proj = matmul(a, w_in_t, n_main, layer=layer)
        ab = matmul(a, w_ab, LANES)
        gates = matmul(a, w_in_t[layer, col_gate:, :], 2 * d)
        pw = pool_w[layer].astype(BF16)
        ya_p = pool_mixer(proj, 0, bp, tp, zeros_pool, pw, pool_scale[layer], 0)
        ya_s = pool_mixer(proj, mp, bs, ts, state_pool[layer], pw, pool_scale[layer], PAST_LEN)
        yb_p, recs_p = delta_mixer(proj, ab, 0, bp, tp, dp, zeros_conv, zeros_rec, 0, conv_w[layer], a_log[layer],
                                   dt_bias[layer], o_norm[layer], recs_p, layer)
        yb_s, recs_s = delta_mixer(proj, ab, mp, bs, ts, dp, state_conv[layer], state_rec, layer, conv_w[layer],
                                   a_log[layer], dt_bias[layer], o_norm[layer], recs_s, layer)
        h, f_bf = merge_out_proj((ya_p, ya_s), (yb_p, yb_s), gates, w_branch_a[layer].astype(BF16),
                                 w_branch_b[layer].astype(BF16), w_out[layer].astype(BF16),
                                 x_pair if layer == 0 else h, norm_ffn[layer])
        if layer == 0:
            ffn_out = dense_ffn(f_bf, ffn_w_gate[0], ffn_w_up[0], ffn_w_down[0])
        else:
            y_p, y_s = moe_layer(h, norm_ffn[layer], router[0], moe_w_gate[0], moe_w_up[0], moe_w_down[0],
                                 norm_final, mp)
        pools_p.append(_tail_rows(zeros_pool, proj, 0, tp, 0))
        pools_s.append(_tail_rows(state_pool[layer], proj, mp, ts, 0))
        convs_p.append(_tail_rows(zeros_conv, proj, 0, tp, dp))
        convs_s.append(_tail_rows(state_conv[layer], proj, mp, ts, dp))
    return (y_p.reshape(bp, tp, d), y_s.reshape(bs, ts, d),
            jnp.stack(pools_p), jnp.stack(convs_p), recs_p,
            jnp.stack(pools_s), jnp.stack(convs_s), recs_s)
```
